```python
import math
import jax, jax.numpy as jnp
from jax import lax
import numpy as np

D_MODEL = 1024
BATCH = 1
SEQ = 16384
DEPTH = 2
DEC_BATCH = 32
DEC_SEQ = 4
PAST_LEN = 16384
PAGE_SIZE = 128

HEAD_DIM = 64
N_HEADS = D_MODEL // HEAD_DIM
N_KV = N_HEADS // 4
Q_PER_KV = N_HEADS // N_KV
CMP_BLOCK = 32
SEL_BLOCK = 64
SEL_RATIO = SEL_BLOCK // CMP_BLOCK
TOP_N = 16
WINDOW = 512
Q_BLOCK = 128
N_BRANCH = 3
Q_WIDTH = N_HEADS * HEAD_DIM
KV_WIDTH = N_BRANCH * 2 * N_KV * HEAD_DIM
GATE_WIDTH = N_BRANCH * N_HEADS
NSA_PROJ = Q_WIDTH + KV_WIDTH + GATE_WIDTH + Q_WIDTH
N_BUCKETS = 32
MAX_DISTANCE = 1024
D_RNN = D_MODEL
N_LRU_BLOCKS = 8
LRU_BLOCK = D_RNN // N_LRU_BLOCKS
CONV_W = 4
LRU_C = 8.0
N_LRU_LAYERS = (DEPTH + 1) // 2
N_NSA_LAYERS = DEPTH // 2
RMS_EPS = 1e-6

kernel_name = 'hawk_nsa_hybrid_step'


def rmsnorm(x, g):
    xf = x.astype(jnp.float32)
    y = xf * lax.rsqrt(jnp.mean(xf * xf, axis=-1, keepdims=True) + RMS_EPS)
    return (y * g.astype(jnp.float32)).astype(x.dtype)


def t5_bucket(dist):
    n = jnp.maximum(dist, 0)
    max_exact = N_BUCKETS // 2
    nf = jnp.maximum(n, 1).astype(jnp.float32)
    large = max_exact + (jnp.log(nf / max_exact) / math.log(MAX_DISTANCE / max_exact)
                         * (N_BUCKETS - max_exact)).astype(jnp.int32)
    large = jnp.minimum(large, N_BUCKETS - 1)
    return jnp.where(n < max_exact, n, large)


def masked_softmax(logits, mask):
    logits = jnp.where(mask, logits, -jnp.inf)
    m = jnp.max(logits, axis=-1, keepdims=True)
    m = jnp.where(jnp.isfinite(m), m, 0.0)
    e = jnp.exp(logits - m)
    den = jnp.sum(e, axis=-1, keepdims=True)
    return e / jnp.where(den > 0, den, 1.0)


def attn_shared(q, k, v, dist, mask, table):
    logits = jnp.einsum('bqgrd,bngd->bgrqn', q, k).astype(jnp.float32)
    nq, nk = dist.shape
    bias = table[t5_bucket(dist)].astype(jnp.float32).reshape(nq, nk, N_KV, Q_PER_KV).transpose(2, 3, 0, 1)
    p = masked_softmax(logits + bias, mask)
    o = jnp.einsum('bgrqn,bngd->bqgrd', p.astype(v.dtype), v)
    return o, p


def attn_gathered(q, k, v, dist, mask, table):
    logits = jnp.einsum('bqgrd,bgqnd->bgrqn', q, k).astype(jnp.float32)
    table_gr = table.reshape(N_BUCKETS, N_KV, Q_PER_KV)
    gidx = jnp.arange(N_KV)[None, :, None, None]
    bias = table_gr[t5_bucket(dist), gidx].astype(jnp.float32).transpose(0, 1, 4, 2, 3)
    p = masked_softmax(logits + bias, mask[:, :, None])
    return jnp.einsum('bgrqn,bgqnd->bqgrd', p.astype(v.dtype), v)


def compress_blocks(kv2, w_cmp):
    b, t = kv2.shape[:2]
    blocks = kv2.reshape(b, t // CMP_BLOCK, CMP_BLOCK, 2, N_KV, HEAD_DIM)
    return jnp.einsum('bclkgd,klde->bckge', blocks, w_cmp)


def select_blocks(p_cmp, t, k_top):
    b, g, r, nq, nc = p_cmp.shape
    imp = jnp.sum(p_cmp, axis=2).reshape(b, g, nq, nc // SEL_RATIO, SEL_RATIO).sum(-1)
    blk = jnp.arange(nc // SEL_RATIO)
    valid = blk[None, :] * SEL_BLOCK <= t[:, None]
    forced = (blk[None, :] == 0) | (blk[None, :] == (t // SEL_BLOCK)[:, None])
    scores = jnp.where(forced, jnp.inf, jnp.where(valid, imp, -jnp.inf))
    _, idx = lax.top_k(scores, k_top)
    return idx


def nsa_project(xn, w_in):
    b, t, _ = xn.shape
    z = xn @ w_in
    q = z[..., :Q_WIDTH].reshape(b, t, N_KV, Q_PER_KV, HEAD_DIM) * (HEAD_DIM ** -0.5)
    kv = z[..., Q_WIDTH:Q_WIDTH + KV_WIDTH].reshape(b, t, N_BRANCH, 2, N_KV, HEAD_DIM)
    o = Q_WIDTH + KV_WIDTH
    gates = jax.nn.sigmoid(z[..., o:o + GATE_WIDTH]).reshape(b, t, N_BRANCH, N_KV, Q_PER_KV)
    u = z[..., o + GATE_WIDTH:]
    return q, kv, gates, u


def merge_branches(gates, o_c, o_s, o_w):
    return (gates[:, :, 0, :, :, None] * o_c + gates[:, :, 1, :, :, None] * o_s
            + gates[:, :, 2, :, :, None] * o_w)


def nsa_prompt(q, kv, gates, w_cmp, table):
    b, s_len = q.shape[:2]
    kc = compress_blocks(kv[:, :, 0], w_cmp)
    ends = jnp.arange(s_len // CMP_BLOCK) * CMP_BLOCK + (CMP_BLOCK - 1)
    k_sel, v_sel = kv[:, :, 1, 0], kv[:, :, 1, 1]
    kw_pad = jnp.pad(kv[:, :, 2], ((0, 0), (WINDOW, 0), (0, 0), (0, 0), (0, 0)))
    k_top = min(TOP_N, s_len // SEL_BLOCK)
    bidx = jnp.arange(b)[:, None, None, None]
    gidx = jnp.arange(N_KV)[None, :, None, None]

    def one_block(q0):
        t = q0 + jnp.arange(Q_BLOCK)
        qb = lax.dynamic_slice_in_dim(q, q0, Q_BLOCK, axis=1)
        o_c, p_c = attn_shared(qb, kc[:, :, 0], kc[:, :, 1], t[:, None] - ends[None, :],
                               ends[None, :] <= t[:, None], table)
        idx = select_blocks(p_c, t, k_top)
        tok = (idx[..., None] * SEL_BLOCK + jnp.arange(SEL_BLOCK)).reshape(b, N_KV, Q_BLOCK, k_top * SEL_BLOCK)
        ks = k_sel[bidx, tok, gidx]
        vs = v_sel[bidx, tok, gidx]
        o_s = attn_gathered(qb, ks, vs, t[:, None] - tok, tok <= t[:, None], table)
        kw = lax.dynamic_slice_in_dim(kw_pad, q0, WINDOW + Q_BLOCK, axis=1)
        s = q0 - WINDOW + jnp.arange(WINDOW + Q_BLOCK)
        d = t[:, None] - s[None, :]
        o_w, _ = attn_shared(qb, kw[:, :, 0], kw[:, :, 1], d,
                             (d >= 0) & (d < WINDOW) & (s[None, :] >= 0), table)
        gb = lax.dynamic_slice_in_dim(gates, q0, Q_BLOCK, axis=1)
        return merge_branches(gb, o_c, o_s, o_w).reshape(b, Q_BLOCK, Q_WIDTH)

    out = lax.map(one_block, jnp.arange(s_len // Q_BLOCK) * Q_BLOCK)
    return out.transpose(1, 0, 2, 3).reshape(b, s_len, Q_WIDTH)


def nsa_sample(q, kv, gates, pool, layer_idx, win_buf, page_table, w_cmp, table):
    nb, ds = q.shape[:2]
    t = PAST_LEN + jnp.arange(ds)
    past = pool[page_table, layer_idx, :, 0:2]
    past = past.reshape(nb, PAST_LEN, 2, N_KV, HEAD_DIM)
    kc = compress_blocks(past, w_cmp)
    ends = jnp.arange(PAST_LEN // CMP_BLOCK) * CMP_BLOCK + (CMP_BLOCK - 1)
    o_c, p_c = attn_shared(q, kc[:, :, 0], kc[:, :, 1], t[:, None] - ends[None, :],
                           ends[None, :] <= t[:, None], table)
    k_top = min(TOP_N - 1, PAST_LEN // SEL_BLOCK)
    idx = select_blocks(p_c, t, k_top)
    tok = (idx[..., None] * SEL_BLOCK + jnp.arange(SEL_BLOCK)).reshape(nb, N_KV, ds, k_top * SEL_BLOCK)
    bidx = jnp.arange(nb)[:, None, None, None]
    gidx = jnp.arange(N_KV)[None, :, None, None]
    phys = page_table[bidx, tok // PAGE_SIZE]
    off = tok % PAGE_SIZE
    k_past = pool[phys, layer_idx, off, 2, gidx]
    v_past = pool[phys, layer_idx, off, 3, gidx]
    new_k = jnp.broadcast_to(kv[:, :, 1, 0].transpose(0, 2, 1, 3)[:, :, None], (nb, N_KV, ds, ds, HEAD_DIM))
    new_v = jnp.broadcast_to(kv[:, :, 1, 1].transpose(0, 2, 1, 3)[:, :, None], (nb, N_KV, ds, ds, HEAD_DIM))
    new_tok = jnp.broadcast_to(t, (nb, N_KV, ds, ds))
    ks = jnp.concatenate([k_past.astype(new_k.dtype), new_k], axis=3)
    vs = jnp.concatenate([v_past.astype(new_v.dtype), new_v], axis=3)
    tok_all = jnp.concatenate([tok, new_tok], axis=-1)
    o_s = attn_gathered(q, ks, vs, t[:, None] - tok_all, tok_all <= t[:, None], table)
    wb = win_buf.shape[1]
    keys = jnp.concatenate([win_buf.astype(kv.dtype), kv[:, :, 2]], axis=1)
    s = PAST_LEN - wb + jnp.arange(wb + ds)
    d = t[:, None] - s[None, :]
    o_w, _ = attn_shared(q, keys[:, :, 0], keys[:, :, 1], d, (d >= 0) & (d < WINDOW), table)
    o = merge_branches(gates, o_c, o_s, o_w).reshape(nb, ds, Q_WIDTH)
    return o, keys[:, -wb:]


def lru_mixer(xn, conv_buf, h0, w_in, conv_w, conv_b, w_a, b_a, w_x, b_x, lam, w_out):
    b, t_len, _ = xn.shape
    z = xn @ w_in
    xb, gate = z[..., :D_RNN], z[..., D_RNN:]
    xp = jnp.concatenate([conv_buf.astype(xb.dtype), xb], axis=1)
    conv = conv_b
    for j in range(CONV_W):
        conv = conv + conv_w[j] * xp[:, j:j + t_len]
    cb = conv.reshape(b, t_len, N_LRU_BLOCKS, LRU_BLOCK)
    r = jax.nn.sigmoid((jnp.einsum('btnc,ncd->btnd', cb, w_a).reshape(b, t_len, D_RNN) + b_a).astype(jnp.float32))
    i = jax.nn.sigmoid((jnp.einsum('btnc,ncd->btnd', cb, w_x).reshape(b, t_len, D_RNN) + b_x).astype(jnp.float32))
    log_a = -LRU_C * r * jax.nn.softplus(-lam.astype(jnp.float32))
    a = jnp.exp(log_a)
    u = jnp.sqrt(-jnp.expm1(2.0 * log_a)) * i * conv.astype(jnp.float32)

    def step(h, au):
        a_t, u_t = au
        h = a_t * h + u_t
        return h, h

    h_last, hs = lax.scan(step, h0.astype(jnp.float32), (a.transpose(1, 0, 2), u.transpose(1, 0, 2)))
    y = hs.transpose(1, 0, 2).astype(xn.dtype) * jax.nn.silu(gate)
    return y @ w_out, xp[:, -(CONV_W - 1):], h_last.astype(xn.dtype)


def setup_inputs(seed: int = 0) -> dict:
    key = jax.random.key(seed)
    k = jax.random.split(key, 24)
    f32 = jnp.float32
    n_pages = PAST_LEN // PAGE_SIZE
    n_used = DEC_BATCH * n_pages
    n_phys = n_used + n_used // 4
    wb = min(WINDOW, PAST_LEN)

    def nrm(kk, shape, scale):
        return scale * jax.random.normal(kk, shape, f32)

    u = jax.random.uniform(k[16], (N_LRU_LAYERS, D_RNN), f32, 0.9, 0.999)
    s = u ** (1.0 / LRU_C)
    return {
        'x_prompt': nrm(k[0], (BATCH, SEQ, D_MODEL), 1.0),
        'x_sample': nrm(k[1], (DEC_BATCH, DEC_SEQ, D_MODEL), 1.0),
        'state_lru_h': nrm(k[2], (N_LRU_LAYERS, DEC_BATCH, D_RNN), 0.5),
        'state_conv': nrm(k[3], (N_LRU_LAYERS, DEC_BATCH, CONV_W - 1, D_RNN), 1.0),
        'cache_nsa_kv': nrm(k[4], (n_phys, N_NSA_LAYERS, PAGE_SIZE, 4, N_KV, HEAD_DIM), 1.0),
        'cache_win_kv': nrm(k[5], (N_NSA_LAYERS, DEC_BATCH, wb, 2, N_KV, HEAD_DIM), 1.0),
        'page_table': jax.random.permutation(k[6], n_phys)[:n_used].reshape(DEC_BATCH, n_pages).astype(jnp.int32),
        'norm_g': 1.0 + nrm(k[7], (DEPTH, D_MODEL), 0.01),
        'final_norm_g': 1.0 + nrm(k[8], (D_MODEL,), 0.01),
        'w_in_lru': nrm(k[9], (N_LRU_LAYERS, D_MODEL, 2 * D_RNN), D_MODEL ** -0.5),
        'conv_w': nrm(k[10], (N_LRU_LAYERS, CONV_W, D_RNN), CONV_W ** -0.5),
        'conv_b': nrm(k[11], (N_LRU_LAYERS, D_RNN), 0.01),
        'w_gate_a': nrm(k[12], (N_LRU_LAYERS, N_LRU_BLOCKS, LRU_BLOCK, LRU_BLOCK), LRU_BLOCK ** -0.5),
        'b_gate_a': nrm(k[13], (N_LRU_LAYERS, D_RNN), 0.01),
        'w_gate_x': nrm(k[14], (N_LRU_LAYERS, N_LRU_BLOCKS, LRU_BLOCK, LRU_BLOCK), LRU_BLOCK ** -0.5),
        'b_gate_x': nrm(k[15], (N_LRU_LAYERS, D_RNN), 0.01),
        'lru_lambda': jnp.log(s) - jnp.log1p(-s),
        'w_out_lru': nrm(k[17], (N_LRU_LAYERS, D_RNN, D_MODEL), D_RNN ** -0.5),
        'w_in_nsa': nrm(k[18], (N_NSA_LAYERS, D_MODEL, NSA_PROJ), D_MODEL ** -0.5),
        'w_cmp': nrm(k[19], (N_NSA_LAYERS, 2, CMP_BLOCK, HEAD_DIM, HEAD_DIM), (CMP_BLOCK * HEAD_DIM) ** -0.5),
        'w_out_nsa': nrm(k[20], (N_NSA_LAYERS, Q_WIDTH, D_MODEL), Q_WIDTH ** -0.5),
        'rel_bias': nrm(k[21], (N_BUCKETS, N_HEADS), 0.1),
    }


def reference(x_prompt, x_sample, state_lru_h, state_conv, cache_nsa_kv, cache_win_kv, page_table,
              norm_g, final_norm_g, w_in_lru, conv_w, conv_b, w_gate_a, b_gate_a, w_gate_x, b_gate_x,
              lru_lambda, w_out_lru, w_in_nsa, w_cmp, w_out_nsa, rel_bias):
    xp, xs = x_prompt, x_sample
    bp, s_len = xp.shape[:2]
    wb_prompt = min(WINDOW, s_len)
    p_h, p_conv, p_rows, p_win = [], [], [], []
    s_h, s_conv, s_rows, s_win = [], [], [], []
    for layer in range(DEPTH):
        j = layer // 2
        hp = rmsnorm(xp, norm_g[layer])
        hs = rmsnorm(xs, norm_g[layer])
        if layer % 2 == 0:
            zero_buf = jnp.zeros((bp, CONV_W - 1, D_RNN), xp.dtype)
            zero_h = jnp.zeros((bp, D_RNN), xp.dtype)
            yp, buf_p, h_p = lru_mixer(hp, zero_buf, zero_h, w_in_lru[j], conv_w[j], conv_b[j], w_gate_a[j],
                                       b_gate_a[j], w_gate_x[j], b_gate_x[j], lru_lambda[j], w_out_lru[j])
            ys, buf_s, h_s = lru_mixer(hs, state_conv[j], state_lru_h[j], w_in_lru[j], conv_w[j], conv_b[j],
                                       w_gate_a[j], b_gate_a[j], w_gate_x[j], b_gate_x[j], lru_lambda[j],
                                       w_out_lru[j])
            p_h.append(h_p); p_conv.append(buf_p)
            s_h.append(h_s); s_conv.append(buf_s)
        else:
            q, kv, gates, u = nsa_project(hp, w_in_nsa[j])
            op = nsa_prompt(q, kv, gates, w_cmp[j], rel_bias)
            yp = (op * jax.nn.silu(u)) @ w_out_nsa[j]
            p_rows.append(kv[:, :, :2].reshape(bp, s_len, 4, N_KV, HEAD_DIM))
            p_win.append(kv[:, -wb_prompt:, 2])
            q, kv, gates, u = nsa_project(hs, w_in_nsa[j])
            os_, new_win = nsa_sample(q, kv, gates, cache_nsa_kv, j, cache_win_kv[j], page_table, w_cmp[j], rel_bias)
            ys = (os_ * jax.nn.silu(u)) @ w_out_nsa[j]
            s_rows.append(kv[:, :, :2].reshape(xs.shape[0], xs.shape[1], 4, N_KV, HEAD_DIM))
            s_win.append(new_win)
        xp = xp + yp
        xs = xs + ys
    y_prompt = rmsnorm(xp, final_norm_g)
    y_sample = rmsnorm(xs, final_norm_g)
    prompt_lru_h = jnp.stack(p_h)
    prompt_conv = jnp.stack(p_conv)
    prompt_kv_rows = jnp.stack(p_rows, axis=1)
    prompt_win_kv = jnp.stack(p_win)
    sample_lru_h = jnp.stack(s_h)
    sample_conv = jnp.stack(s_conv)
    sample_kv_rows = jnp.stack(s_rows, axis=1)
    sample_win_kv = jnp.stack(s_win)
    return (y_prompt, y_sample, prompt_lru_h, prompt_conv, prompt_kv_rows, prompt_win_kv,
            sample_lru_h, sample_conv, sample_kv_rows, sample_win_kv)
```

```python
import functools
import math

import jax
import jax.numpy as jnp
from jax import lax
from jax.experimental import pallas as pl
from jax.experimental.pallas import tpu as pltpu

F32 = jnp.float32
BF16 = jnp.bfloat16

D_MODEL = 1024
HEAD_DIM = 64
N_HEADS = 16
N_KV = 4
Q_PER_KV = 4
CMP_BLOCK = 32
SEL_BLOCK = 64
TOP_N = 16
WINDOW = 512
PAGE_SIZE = 128
N_BUCKETS = 32
MAX_DISTANCE = 1024
N_LRU_BLOCKS = 8
LRU_BLOCK = 128
CONV_W = 4
LRU_C = 8.0
RMS_EPS = 1e-6

V7X_VMEM_BYTES = 64 * 1024 * 1024
VMEM_LIMIT = V7X_VMEM_BYTES - 8 * 1024 * 1024


def _bucket_lower_bounds():
    max_exact = N_BUCKETS // 2
    lows = list(range(1, max_exact + 1))
    for k in range(1, N_BUCKETS - max_exact):
        n = max_exact
        while n ** 8 < 2 ** (32 + 3 * k):
            n += 1
        lows.append(n)
    return lows


_BUCKET_LOWS = _bucket_lower_bounds()


def _bias_from_dist(dist, table_col):
    val = jnp.full(dist.shape, table_col(0), F32)
    for b, lo in enumerate(_BUCKET_LOWS, start=1):
        val = jnp.where(dist >= lo, table_col(b), val)
    return val


def _rms(x, g):
    ms = jnp.mean(x * x, axis=-1, keepdims=True)
    return x * lax.rsqrt(ms + RMS_EPS) * g


def _sigmoid(x):
    return 1.0 / (1.0 + jnp.exp(-x))


def _silu(x):
    return x * _sigmoid(x)


def _softplus(x):
    return jnp.maximum(x, 0.0) + jnp.log1p(jnp.exp(-jnp.abs(x)))


def _expm1(x):
    u = jnp.exp(x)
    um1 = u - 1.0
    y = um1 * x / jnp.log(u)
    y = jnp.where(u == 1.0, x, y)
    return jnp.where(um1 == -1.0, -1.0, y)


def _cparams(*sem):
    return pltpu.CompilerParams(dimension_semantics=sem, vmem_limit_bytes=VMEM_LIMIT)


def _lru_gate_terms(conv, wa_ref, ba, wx_ref, bx, lam):
    cb = conv.astype(BF16)
    r_parts, i_parts = [], []
    for n in range(N_LRU_BLOCKS):
        c = cb[:, n * LRU_BLOCK:(n + 1) * LRU_BLOCK]
        r_parts.append(jnp.dot(c, wa_ref[n], preferred_element_type=F32))
        i_parts.append(jnp.dot(c, wx_ref[n], preferred_element_type=F32))
    r = _sigmoid(jnp.concatenate(r_parts, axis=1) + ba)
    i = _sigmoid(jnp.concatenate(i_parts, axis=1) + bx)
    log_a = -LRU_C * r * _softplus(-lam)
    a = jnp.exp(log_a)
    u = jnp.sqrt(-_expm1(2.0 * log_a)) * i * conv
    return a, u


def _lru_prompt_kernel(x_ref, g_ref, win_ref, cw_ref, cb_ref, wa_ref, ba_ref, wx_ref, bx_ref,
                       lam_ref, wout_ref, x1_ref, hlast_ref, tail_ref,
                       xb_ext, a_s, u_s, h_c, *, tr):
    i = pl.program_id(0)

    @pl.when(i == 0)
    def _():
        xb_ext[0:8, :] = jnp.zeros((8, D_MODEL), F32)
        h_c[...] = jnp.zeros_like(h_c)

    x = x_ref[...]
    xn = _rms(x, g_ref[...])
    z = jnp.dot(xn.astype(BF16), win_ref[...], preferred_element_type=F32)
    xb = z[:, :D_MODEL]
    gate = z[:, D_MODEL:]
    xb_ext[8:8 + tr, :] = xb
    conv = cb_ref[...]
    for j in range(CONV_W):
        conv = conv + cw_ref[j:j + 1, :] * xb_ext[8 - (CONV_W - 1) + j:8 - (CONV_W - 1) + j + tr, :]
    a, u = _lru_gate_terms(conv, wa_ref, ba_ref[...], wx_ref, bx_ref[...], lam_ref[...])
    a_s[...] = a
    u_s[...] = u

    row = lax.broadcasted_iota(jnp.int32, (8, D_MODEL), 0)

    def body(k, h):
        base = pl.multiple_of(k * 8, 8)
        aa = a_s[pl.ds(base, 8), :]
        uu = u_s[pl.ds(base, 8), :]
        for s in (1, 2, 4):
            keep = row >= s
            us = jnp.where(keep, pltpu.roll(uu, s, axis=0), 0.0)
            as_ = jnp.where(keep, pltpu.roll(aa, s, axis=0), 1.0)
            uu = uu + aa * us
            aa = aa * as_
        hb = aa * h + uu
        u_s[pl.ds(base, 8), :] = hb
        return hb[7:8, :]

    h_last = lax.fori_loop(0, tr // 8, body, h_c[...])
    h_c[...] = h_last
    hlast_ref[...] = h_last
    tail_ref[...] = xb[tr - 8:, :]
    xb_ext[0:8, :] = xb[tr - 8:, :]
    y = u_s[...] * _silu(gate)
    out = jnp.dot(y.astype(BF16), wout_ref[...], preferred_element_type=F32)
    x1_ref[...] = x + out


def _lru_prompt(x, g, w_in, cw, cb, wa, ba, wx, bx, lam, w_out, tr=512):
    s = x.shape[0]
    tr = min(tr, s)
    assert s % tr == 0 and tr % 8 == 0
    full = lambda shp: pl.BlockSpec(shp, lambda i: (0,) * len(shp))
    return pl.pallas_call(
        functools.partial(_lru_prompt_kernel, tr=tr),
        grid=(s // tr,),
        in_specs=[pl.BlockSpec((tr, D_MODEL), lambda i: (i, 0)), full((1, D_MODEL)),
                  full((D_MODEL, 2 * D_MODEL)), full((CONV_W, D_MODEL)), full((1, D_MODEL)),
                  full((N_LRU_BLOCKS, LRU_BLOCK, LRU_BLOCK)), full((1, D_MODEL)),
                  full((N_LRU_BLOCKS, LRU_BLOCK, LRU_BLOCK)), full((1, D_MODEL)),
                  full((1, D_MODEL)), full((D_MODEL, D_MODEL))],
        out_specs=[pl.BlockSpec((tr, D_MODEL), lambda i: (i, 0)), full((1, D_MODEL)), full((8, D_MODEL))],
        out_shape=[jax.ShapeDtypeStruct((s, D_MODEL), F32), jax.ShapeDtypeStruct((1, D_MODEL), F32),
                   jax.ShapeDtypeStruct((8, D_MODEL), F32)],
        scratch_shapes=[pltpu.VMEM((tr + 8, D_MODEL), F32), pltpu.VMEM((tr, D_MODEL), F32),
                        pltpu.VMEM((tr, D_MODEL), F32), pltpu.VMEM((1, D_MODEL), F32)],
        compiler_params=_cparams("arbitrary"),
        name="lru_prompt",
    )(x, g, w_in, cw, cb, wa, ba, wx, bx, lam, w_out)


def _lru_sample_kernel(x_ref, buf_ref, h0_ref, g_ref, win_ref, cw_ref, cb_ref, wa_ref, ba_ref, wx_ref,
                       bx_ref, lam_ref, wout_ref, x1_ref, hlast_ref, tail_ref, xp_s, *, nb, ds):
    x = x_ref[...]
    xn = _rms(x, g_ref[...])
    z = jnp.dot(xn.astype(BF16), win_ref[...], preferred_element_type=F32)
    xb = z[:, :D_MODEL]
    gate = z[:, D_MODEL:]
    nbuf = (CONV_W - 1) * nb
    xp_s[0:nbuf, :] = buf_ref[...]
    xp_s[nbuf:, :] = xb
    conv = cb_ref[...]
    for j in range(CONV_W):
        conv = conv + cw_ref[j:j + 1, :] * xp_s[j * nb:j * nb + ds * nb, :]
    a, u = _lru_gate_terms(conv, wa_ref, ba_ref[...], wx_ref, bx_ref[...], lam_ref[...])
    h = h0_ref[...]
    hs = []
    for t in range(ds):
        h = a[t * nb:(t + 1) * nb, :] * h + u[t * nb:(t + 1) * nb, :]
        hs.append(h)
    hlast_ref[...] = h
    tail_ref[...] = xp_s[ds * nb:, :]
    y = jnp.concatenate(hs, axis=0) * _silu(gate)
    out = jnp.dot(y.astype(BF16), wout_ref[...], preferred_element_type=F32)
    x1_ref[...] = x + out


def _lru_sample(x_tm, buf_tm, h0, g, w_in, cw, cb, wa, ba, wx, bx, lam, w_out, nb, ds):
    rows = nb * ds
    nbuf = (CONV_W - 1) * nb
    return pl.pallas_call(
        functools.partial(_lru_sample_kernel, nb=nb, ds=ds),
        out_shape=[jax.ShapeDtypeStruct((rows, D_MODEL), F32), jax.ShapeDtypeStruct((nb, D_MODEL), F32),
                   jax.ShapeDtypeStruct((nbuf, D_MODEL), F32)],
        scratch_shapes=[pltpu.VMEM((nbuf + rows, D_MODEL), F32)],
        compiler_params=pltpu.CompilerParams(vmem_limit_bytes=VMEM_LIMIT),
        name="lru_sample",
    )(x_tm, buf_tm, h0, g, w_in, cw, cb, wa, ba, wx, bx, lam, w_out)


Q_BLOCK = 128
LANES_G = Q_PER_KV * Q_BLOCK
KEY_TILE = 256
NEAR_PAD = 256
NEAR_ROWS = NEAR_PAD + MAX_DISTANCE + Q_BLOCK + Q_BLOCK
NEG_INF = float("-inf")
M_FLOOR = -1e30

KV_WIDTH = 3 * 2 * N_KV * HEAD_DIM
GATE_PAD = 128


def _proj_prompt_kernel(x_ref, g_ref, wn_ref, wt_ref, cmp_ref, selk_ref, wink_ref, u_ref,
                        qt_ref, kvcst_ref, kvwt_ref, selvt_ref, winvt_ref, gt_ref, *, tr):
    xn = _rms(x_ref[...], g_ref[...]).astype(BF16)
    z = jnp.dot(xn, wn_ref[...], preferred_element_type=F32)
    cmp_ref[...] = z[:, 0:512].astype(BF16)
    for p in range(2):
        selk_ref[p] = z[:, 512 + 128 * p:640 + 128 * p].astype(BF16)
        wink_ref[p] = z[:, 768 + 128 * p:896 + 128 * p].astype(BF16)
    u_ref[...] = z[:, 1024:2048]
    zt = lax.dot_general(wt_ref[...], xn, (((1,), (1,)), ((), ())), preferred_element_type=F32)
    qt_ref[...] = (zt[0:1024] * (HEAD_DIM ** -0.5)).astype(BF16).reshape(N_KV, Q_PER_KV * HEAD_DIM, tr)
    kvt = zt[1024:1024 + KV_WIDTH]
    kvcst_ref[...] = kvt[0:1024]
    kvwt_ref[...] = kvt[1024:1536]
    selvt_ref[...] = kvt[768:1024].astype(BF16).reshape(N_KV, HEAD_DIM, tr)
    winvt_ref[...] = kvt[1280:1536].astype(BF16).reshape(N_KV, HEAD_DIM, tr)
    gt_ref[...] = _sigmoid(zt[1024 + KV_WIDTH:1024 + KV_WIDTH + GATE_PAD])


def _proj_prompt(x1, g, w_nat, w_t, tr=512):
    s = x1.shape[0]
    tr = min(tr, s)
    nn, nt = w_nat.shape[1], w_t.shape[0]
    full = lambda shp: pl.BlockSpec(shp, lambda i: (0,) * len(shp))
    sds = jax.ShapeDtypeStruct
    return pl.pallas_call(
        functools.partial(_proj_prompt_kernel, tr=tr),
        grid=(s // tr,),
        in_specs=[pl.BlockSpec((tr, D_MODEL), lambda i: (i, 0)), full((1, D_MODEL)), full((D_MODEL, nn)),
                  full((nt, D_MODEL))],
        out_specs=[pl.BlockSpec((tr, 512), lambda i: (i, 0)),
                   pl.BlockSpec((2, tr, 128), lambda i: (0, i, 0)), pl.BlockSpec((2, tr, 128), lambda i: (0, i, 0)),
                   pl.BlockSpec((tr, 1024), lambda i: (i, 0)),
                   pl.BlockSpec((N_KV, 256, tr), lambda i: (0, 0, i)),
                   pl.BlockSpec((1024, tr), lambda i: (0, i)), pl.BlockSpec((512, tr), lambda i: (0, i)),
                   pl.BlockSpec((N_KV, HEAD_DIM, tr), lambda i: (0, 0, i)),
                   pl.BlockSpec((N_KV, HEAD_DIM, tr), lambda i: (0, 0, i)),
                   pl.BlockSpec((GATE_PAD, tr), lambda i: (0, i))],
        out_shape=[sds((s, 512), BF16), sds((2, s, 128), BF16), sds((2, s, 128), BF16),
                   sds((s, 1024), F32), sds((N_KV, 256, s), BF16), sds((1024, s), F32), sds((512, s), F32),
                   sds((N_KV, HEAD_DIM, s), BF16), sds((N_KV, HEAD_DIM, s), BF16), sds((GATE_PAD, s), F32)],
        compiler_params=_cparams("arbitrary"),
        name="nsa_proj_prompt",
    )(x1, g, w_nat, w_t)


def _compress_prompt_kernel(xe_ref, xo_ref, wk_ref, wvt_ref, kck_ref, kcvt_ref, acck, accv, *, tl, nh):
    step = pl.program_id(0)

    @pl.when(step == 0)
    def _():
        acck[...] = jnp.zeros_like(acck)
        accv[...] = jnp.zeros_like(accv)

    for half, x_ref in enumerate((xe_ref, xo_ref)):
        for l in range(tl):
            xk = x_ref[:, l * 512:l * 512 + 256]
            xv = x_ref[:, l * 512 + 256:l * 512 + 512]
            acck[half * nh:(half + 1) * nh, :] += jnp.dot(xk, wk_ref[l], preferred_element_type=F32)
            accv[:, half * nh:(half + 1) * nh] += lax.dot_general(
                wvt_ref[l], xv, (((1,), (1,)), ((), ())), preferred_element_type=F32)

    @pl.when(step == pl.num_programs(0) - 1)
    def _():
        for p in range(2):
            kck_ref[p] = acck[:, 128 * p:128 * (p + 1)].astype(BF16)
        kcvt_ref[...] = accv[...].astype(BF16).reshape(N_KV, HEAD_DIM, 2 * nh)


def _compress_prompt(cmp_kv, wk_bd, wvt_bd, tl=4):
    s = cmp_kv.shape[0]
    nh = s // (2 * CMP_BLOCK)
    x2 = cmp_kv.reshape(nh, 2 * CMP_BLOCK * 512)
    nsteps = CMP_BLOCK // tl
    return pl.pallas_call(
        functools.partial(_compress_prompt_kernel, tl=tl, nh=nh),
        grid=(nsteps,),
        in_specs=[pl.BlockSpec((nh, tl * 512), lambda i: (0, i)),
                  pl.BlockSpec((nh, tl * 512), lambda i: (0, nsteps + i)),
                  pl.BlockSpec((tl, 256, 256), lambda i: (i, 0, 0)),
                  pl.BlockSpec((tl, 256, 256), lambda i: (i, 0, 0))],
        out_specs=[pl.BlockSpec((2, 2 * nh, 128), lambda i: (0, 0, 0)),
                   pl.BlockSpec((N_KV, HEAD_DIM, 2 * nh), lambda i: (0, 0, 0))],
        out_shape=[jax.ShapeDtypeStruct((2, 2 * nh, 128), BF16),
                   jax.ShapeDtypeStruct((N_KV, HEAD_DIM, 2 * nh), BF16)],
        scratch_shapes=[pltpu.VMEM((2 * nh, 256), F32), pltpu.VMEM((256, 2 * nh), F32)],
        compiler_params=_cparams("arbitrary"),
        name="nsa_compress_prompt",
    )(x2, x2, wk_bd, wvt_bd)


def _softmax_rows(s):
    m = jnp.max(s, axis=0, keepdims=True)
    m = jnp.where(m == NEG_INF, 0.0, m)
    e = jnp.exp(s - m)
    den = jnp.sum(e, axis=0, keepdims=True)
    return e / jnp.where(den > 0, den, 1.0)


def _attn_prompt_kernel(bias_ref, qt_ref, kck_ref, kcvt_ref, selk_ref, selvt_ref,
                        wk0, wk1, wk2, wk3, wk4, wv0, wv1, wv2, wv3, wv4,
                        gt_ref, u_ref, x1_ref, wout_ref, gfin_ref, y_ref,
                        nb_s, wq_s, pen_s, sc_s, obr_s, opt_s, *, s_len):
    i = pl.program_id(0)
    q0 = i * Q_BLOCK
    nc = s_len // CMP_BLOCK
    nch = nc // 2
    nblk = s_len // SEL_BLOCK
    k_top = min(TOP_N, nblk)
    wrows = min(32, nch)
    wk_refs = (wk0, wk1, wk2, wk3, wk4)
    wv_refs = (wv0, wv1, wv2, wv3, wv4)
    n_win = len(wk_refs)

    def table(h):
        return lambda b: bias_ref[b, h] - bias_ref[N_BUCKETS - 1, h]

    @pl.when(i == 0)
    def _():
        wq_s[...] = jnp.zeros_like(wq_s)
        rho = lax.broadcasted_iota(jnp.int32, (NEAR_ROWS, Q_BLOCK), 0)
        a = lax.broadcasted_iota(jnp.int32, (NEAR_ROWS, Q_BLOCK), 1)
        dist = a + (MAX_DISTANCE + NEAR_PAD) - rho
        for h in range(N_HEADS):
            nb_s[h] = _bias_from_dist(dist, table(h))

    for g in range(N_KV):
        off = (g % 2) * HEAD_DIM
        for r in range(Q_PER_KV):
            wq_s[g, off:off + HEAD_DIM, r * Q_BLOCK:(r + 1) * Q_BLOCK] = qt_ref[g, r * HEAD_DIM:(r + 1) * HEAD_DIM, :]

    lane_q = lax.broadcasted_iota(jnp.int32, (1, LANES_G), 1) % Q_BLOCK
    t_lane = q0 + lane_q

    def group_body(g, carry):
        wq = wq_s[g]
        pair = g // 2

        sc_s[...] = jnp.dot(kck_ref[pair], wq, preferred_element_type=F32)
        w0 = jnp.clip(((2 * i - 16) // 8) * 8, 0, nch - wrows)
        w0 = pl.multiple_of(w0, 8)
        jj = lax.broadcasted_iota(jnp.int32, (wrows, Q_BLOCK), 0) + w0
        aq = lax.broadcasted_iota(jnp.int32, (wrows, Q_BLOCK), 1) + q0
        for half in range(2):
            dist = aq - SEL_BLOCK * jj - (CMP_BLOCK * half + CMP_BLOCK - 1)
            rows = pl.ds(pl.multiple_of(half * nch + w0, 8), wrows)
            for r in range(Q_PER_KV):
                sc_s[rows, r * Q_BLOCK:(r + 1) * Q_BLOCK] += _bias_from_dist(dist, table(g * Q_PER_KV + r))
        rho = lax.broadcasted_iota(jnp.int32, (nc, 1), 0)
        ends = jnp.where(rho >= nch, SEL_BLOCK * (rho - nch) + 2 * CMP_BLOCK - 1, SEL_BLOCK * rho + CMP_BLOCK - 1)
        p_c = _softmax_rows(jnp.where(ends <= t_lane, sc_s[...], NEG_INF))
        obr_s[0, g] = jnp.dot(kcvt_ref[g], p_c.astype(BF16), preferred_element_type=F32)

        psum = p_c[:, 0:Q_BLOCK]
        for r in range(1, Q_PER_KV):
            psum = psum + p_c[:, r * Q_BLOCK:(r + 1) * Q_BLOCK]
        imp = psum[0:nch] + psum[nch:nc]
        blk = lax.broadcasted_iota(jnp.int32, (nblk, Q_BLOCK), 0)
        tq = lax.broadcasted_iota(jnp.int32, (nblk, Q_BLOCK), 1) + q0
        forced = (blk == 0) | (blk == tq // SEL_BLOCK)
        valid = blk * SEL_BLOCK <= tq
        scores = jnp.where(forced, jnp.inf, jnp.where(valid, imp, NEG_INF))

        blk_f = blk.astype(F32)

        def pick(_, sc_sel):
            sc, sel = sc_sel
            m = jnp.max(sc, axis=0, keepdims=True)
            idx = jnp.min(jnp.where(sc == m, blk_f, float(nblk)), axis=0, keepdims=True)
            hit = blk_f == idx
            return jnp.where(hit, NEG_INF, sc), jnp.where(hit, 1.0, sel)

        _, sel = lax.fori_loop(0, k_top, pick, (scores, jnp.zeros((nblk, Q_BLOCK), F32)))
        pen = jnp.where(sel > 0, 0.0, NEG_INF)
        pen4 = jnp.concatenate([pen] * Q_PER_KV, axis=1)
        pen_s[...] = jnp.broadcast_to(pen4[:, None, :], (nblk, 8, LANES_G))

        def tile(tt, mla, near):
            m_old, l_old, acc = mla
            k0 = pl.multiple_of(tt * KEY_TILE, KEY_TILE)
            s = jnp.dot(selk_ref[pair, pl.ds(k0, KEY_TILE), :], wq, preferred_element_type=F32)
            if near:
                r0 = pl.multiple_of(jnp.maximum(k0 - q0 + MAX_DISTANCE + NEAR_PAD, 0), Q_BLOCK)
                bias = jnp.concatenate(
                    [nb_s[g * Q_PER_KV + r, pl.ds(r0, KEY_TILE), :] for r in range(Q_PER_KV)], axis=1)
                key = k0 + lax.broadcasted_iota(jnp.int32, (KEY_TILE, 1), 0)
                s = jnp.where(key <= t_lane, s + bias, NEG_INF)
            nb4 = KEY_TILE // SEL_BLOCK
            pen_t = pen_s[pl.ds(tt * nb4, nb4)]
            s = (s.reshape(nb4, SEL_BLOCK // 8, 8, LANES_G) + pen_t[:, None]).reshape(KEY_TILE, LANES_G)
            m_new = jnp.maximum(m_old, jnp.max(s, axis=0, keepdims=True))
            alpha = jnp.exp(m_old - m_new)
            p = jnp.exp(s - m_new)
            l_new = alpha * l_old + jnp.sum(p, axis=0, keepdims=True)
            pv = jnp.dot(selvt_ref[g, :, pl.ds(k0, KEY_TILE)], p.astype(BF16), preferred_element_type=F32)
            return m_new, l_new, alpha * acc + pv

        n_tiles = (q0 + Q_BLOCK - 1) // KEY_TILE + 1
        n_far = jnp.maximum(q0 - MAX_DISTANCE, 0) // KEY_TILE
        init = (jnp.full((1, LANES_G), M_FLOOR, F32), jnp.zeros((1, LANES_G), F32),
                jnp.zeros((HEAD_DIM, LANES_G), F32))
        mla = lax.fori_loop(0, n_far, lambda tt, c: tile(tt, c, False), init)
        _, l_s, acc_s = lax.fori_loop(n_far, n_tiles, lambda tt, c: tile(tt, c, True), mla)
        obr_s[1, g] = acc_s / l_s

        kw = jnp.concatenate([wk_refs[k][pair] for k in range(n_win)], axis=0)
        s = jnp.dot(kw, wq, preferred_element_type=F32)
        nw = n_win * Q_BLOCK
        r_w = NEAR_PAD + MAX_DISTANCE - WINDOW
        bias = jnp.concatenate([nb_s[g * Q_PER_KV + r, r_w:r_w + nw, :] for r in range(Q_PER_KV)], axis=1)
        loc = lax.broadcasted_iota(jnp.int32, (nw, 1), 0)
        dist = lane_q + WINDOW - loc
        ok = (dist >= 0) & (dist < WINDOW) & (q0 - WINDOW + loc >= 0)
        p_w = _softmax_rows(jnp.where(ok, s + bias, NEG_INF))
        vw = jnp.concatenate([wv_refs[k][g] for k in range(n_win)], axis=1)
        obr_s[2, g] = jnp.dot(vw, p_w.astype(BF16), preferred_element_type=F32)
        return carry

    lax.fori_loop(0, N_KV, group_body, 0)

    for g in range(N_KV):
        for r in range(Q_PER_KV):
            h = g * Q_PER_KV + r
            sl = slice(r * Q_BLOCK, (r + 1) * Q_BLOCK)
            merged = (gt_ref[h:h + 1, :] * obr_s[0, g, :, sl] + gt_ref[N_HEADS + h:N_HEADS + h + 1, :] * obr_s[1, g, :, sl]
                      + gt_ref[2 * N_HEADS + h:2 * N_HEADS + h + 1, :] * obr_s[2, g, :, sl])
            opt_s[h * HEAD_DIM:(h + 1) * HEAD_DIM, :] = merged
    op = opt_s[...].T
    y = jnp.dot((op * _silu(u_ref[...])).astype(BF16), wout_ref[...], preferred_element_type=F32)
    y_ref[...] = _rms(x1_ref[...] + y, gfin_ref[...])


def _attn_prompt(rel_bias, qt, kck, kcvt, selk, selvt, wink, winvt, gt, u, x1, w_out, gfin):
    s = x1.shape[0]
    nq = s // Q_BLOCK
    nc = s // CMP_BLOCK
    nblk = s // SEL_BLOCK
    assert s % (2 * CMP_BLOCK * 128) == 0 and nblk >= TOP_N
    n_win = WINDOW // Q_BLOCK + 1
    full = lambda shp, **kw: pl.BlockSpec(shp, lambda i: (0,) * len(shp), **kw)
    once = dict(pipeline_mode=pl.Buffered(1))
    win_k_specs = [pl.BlockSpec((2, Q_BLOCK, 128), functools.partial(lambda i, k: (0, jnp.maximum(i - (n_win - 1) + k, 0), 0), k=k))
                   for k in range(n_win)]
    win_v_specs = [pl.BlockSpec((N_KV, HEAD_DIM, Q_BLOCK), functools.partial(lambda i, k: (0, 0, jnp.maximum(i - (n_win - 1) + k, 0)), k=k))
                   for k in range(n_win)]
    return pl.pallas_call(
        functools.partial(_attn_prompt_kernel, s_len=s),
        grid=(nq,),
        in_specs=[pl.BlockSpec(memory_space=pltpu.SMEM),
                  pl.BlockSpec((N_KV, 256, Q_BLOCK), lambda i: (0, 0, i)),
                  full((2, nc, 128), **once), full((N_KV, HEAD_DIM, nc), **once),
                  full((2, s, 128), **once), full((N_KV, HEAD_DIM, s), **once),
                  *win_k_specs, *win_v_specs,
                  pl.BlockSpec((GATE_PAD, Q_BLOCK), lambda i: (0, i)),
                  pl.BlockSpec((Q_BLOCK, 1024), lambda i: (i, 0)),
                  pl.BlockSpec((Q_BLOCK, D_MODEL), lambda i: (i, 0)),
                  full((1024, D_MODEL), **once), full((1, D_MODEL))],
        out_specs=pl.BlockSpec((Q_BLOCK, D_MODEL), lambda i: (i, 0)),
        out_shape=jax.ShapeDtypeStruct((s, D_MODEL), F32),
        scratch_shapes=[pltpu.VMEM((N_HEADS, NEAR_ROWS, Q_BLOCK), F32),
                        pltpu.VMEM((N_KV, 128, LANES_G), BF16),
                        pltpu.VMEM((nblk, 8, LANES_G), F32),
                        pltpu.VMEM((nc, LANES_G), F32),
                        pltpu.VMEM((3, N_KV, HEAD_DIM, LANES_G), F32),
                        pltpu.VMEM((N_HEADS * HEAD_DIM, Q_BLOCK), F32)],
        compiler_params=_cparams("arbitrary"),
        name="nsa_attn_prompt",
    )(rel_bias, qt, kck, kcvt, selk, selvt, *([wink] * n_win), *([winvt] * n_win), gt, u, x1, w_out, gfin)


PAGES_PER_STEP = 4
CHUNK_PAGES = 32
STEPS_PER_CHUNK = CHUNK_PAGES // PAGES_PER_STEP
NEAR_PAGES = MAX_DISTANCE // PAGE_SIZE
NEW_PAD = 128


def _proj_sample_kernel(x_ref, g_ref, wn_ref, kvcs_ref, kvw_ref, q_ref, u_ref, gate_ref):
    xn = _rms(x_ref[...], g_ref[...]).astype(BF16)
    z = jnp.dot(xn, wn_ref[...], preferred_element_type=F32)
    kvcs_ref[...] = z[:, 0:1024]
    kvw_ref[...] = z[:, 1024:1536]
    q_ref[...] = (z[:, 1536:2560] * (HEAD_DIM ** -0.5)).astype(BF16)
    u_ref[...] = z[:, 2560:3584]
    gate_ref[...] = _sigmoid(z[:, 3584:3584 + GATE_PAD])


def _proj_sample(x1, g, w_nat):
    rows = x1.shape[0]
    sds = jax.ShapeDtypeStruct
    return pl.pallas_call(
        _proj_sample_kernel,
        out_shape=[sds((rows, 1024), F32), sds((rows, 512), F32), sds((rows, 1024), BF16),
                   sds((rows, 1024), F32), sds((rows, GATE_PAD), F32)],
        compiler_params=pltpu.CompilerParams(vmem_limit_bytes=VMEM_LIMIT),
        name="nsa_proj_sample",
    )(x1, g, w_nat)


def _softmax_lanes_online(s, m_old, l_old):
    m_new = jnp.maximum(m_old, jnp.max(s, axis=1, keepdims=True))
    alpha = jnp.exp(m_old - m_new)
    p = jnp.exp(s - m_new)
    return m_new, alpha, alpha * l_old + jnp.sum(p, axis=1, keepdims=True), p


def _attn_sample_kernel(pt_ref, pg0, pg1, pg2, pg3, q_ref, new_ref, wint_ref, gate_ref, u_ref, x1_ref,
                        perm_ref, wkt_ref, wv_ref, tb_ref, wout_ref, gfin_ref, y_ref,
                        xl_s, kckt_s, kcv_s, skt_s, svt_s, bc_s, bs_s, bnew_s, bw_s, obr_s,
                        *, past, ds, wb):
    b = pl.program_id(0)
    pg = pl.program_id(1)
    npg = pl.num_programs(1)
    pages = (pg0, pg1, pg2, pg3)
    rows = Q_PER_KV * N_KV * ds
    nchunk = past // (CHUNK_PAGES * PAGE_SIZE)
    npages = past // PAGE_SIZE
    ncl = past // CMP_BLOCK
    nblk = past // SEL_BLOCK
    k_top = min(TOP_N - 1, nblk)
    nt = (((1,), (1,)), ((), ()))

    row_i = lax.broadcasted_iota(jnp.int32, (rows, 1), 0)
    tt = row_i % ds

    @pl.when((b == 0) & (pg == 0))
    def _():
        tbs = tb_ref[...] - tb_ref[:, N_BUCKETS - 1:N_BUCKETS]
        col = lambda k: tbs[:, k:k + 1]
        lane = lax.broadcasted_iota(jnp.int32, (1, ncl), 1)
        slot = lane % 8
        cblk = 8 * (lane // 8) + jnp.where(slot < 4, 2 * slot, 2 * (slot - 4) + 1)
        bc_s[...] = _bias_from_dist(past + tt - (CMP_BLOCK * cblk + CMP_BLOCK - 1), col)
        lane_p = lax.broadcasted_iota(jnp.int32, (1, PAGE_SIZE), 1)
        for k in range(NEAR_PAGES):
            bs_s[k] = _bias_from_dist(tt + MAX_DISTANCE - PAGE_SIZE * k - lane_p, col)
        bnew_s[...] = _bias_from_dist(tt - lax.broadcasted_iota(jnp.int32, (1, NEW_PAD), 1), col)
        bw_s[...] = _bias_from_dist(tt + wb - lax.broadcasted_iota(jnp.int32, (1, wb), 1), col)

    for pair in range(PAGES_PER_STEP // 2):
        pa, pb = pages[2 * pair], pages[2 * pair + 1]
        slab = pl.ds(pl.multiple_of(((pg % STEPS_PER_CHUNK) * (PAGES_PER_STEP // 2) + pair) * 8, 8), 8)
        for kind in range(2):
            xt2 = jnp.concatenate([pa[0, kind * 256:(kind + 1) * 256, :], pb[0, kind * 256:(kind + 1) * 256, :]],
                                  axis=1).astype(BF16)
            x_perm = lax.dot_general(perm_ref[...], xt2, nt, preferred_element_type=F32)
            xl_s[:, slab, kind * 256:(kind + 1) * 256] = x_perm.reshape(CMP_BLOCK, 8, 256)
    for k in range(PAGES_PER_STEP):
        skt_s[pg * PAGES_PER_STEP + k] = pages[k][0, 512:768, :].astype(BF16)
        svt_s[pg * PAGES_PER_STEP + k] = pages[k][0, 768:1024, :].astype(BF16)

    @pl.when(pg % STEPS_PER_CHUNK == STEPS_PER_CHUNK - 1)
    def _():
        nbc = CHUNK_PAGES * PAGE_SIZE // CMP_BLOCK

        def body(l, acc):
            ak, av = acc
            xk = xl_s[l, :, 0:256].astype(BF16)
            xv = xl_s[l, :, 256:512].astype(BF16)
            ak = ak + lax.dot_general(wkt_ref[l], xk, nt, preferred_element_type=F32)
            av = av + jnp.dot(xv, wv_ref[l], preferred_element_type=F32)
            return ak, av

        ak, av = lax.fori_loop(0, CMP_BLOCK, body, (jnp.zeros((256, nbc), F32), jnp.zeros((nbc, 256), F32)))
        ch = pg // STEPS_PER_CHUNK
        kckt_s[ch] = ak.astype(BF16)
        kcv_s[ch] = av.astype(BF16)

    @pl.when(pg == npg - 1)
    def _():
        lane_g = lax.broadcasted_iota(jnp.int32, (1, 256), 1) // HEAD_DIM
        row_g = (row_i // ds) % N_KV
        wq = jnp.where(row_g == lane_g, q_ref[0], jnp.zeros((), BF16))
        t_row = past + tt

        s_c = jnp.concatenate([jnp.dot(wq, kckt_s[ch], preferred_element_type=F32) for ch in range(nchunk)],
                              axis=1) + bc_s[...]
        m = jnp.max(s_c, axis=1, keepdims=True)
        e = jnp.exp(s_c - m)
        p_c = e / jnp.sum(e, axis=1, keepdims=True)
        o_c = jnp.zeros((rows, 256), F32)
        for ch in range(nchunk):
            o_c = o_c + jnp.dot(p_c[:, ch * 128:(ch + 1) * 128].astype(BF16), kcv_s[ch], preferred_element_type=F32)
        obr_s[0] = o_c

        gt_rows = N_KV * ds
        ps = p_c[0:gt_rows]
        for r in range(1, Q_PER_KV):
            ps = ps + p_c[r * gt_rows:(r + 1) * gt_rows]
        imp = ps + pltpu.roll(ps, ncl - 4, axis=1)
        lane = lax.broadcasted_iota(jnp.int32, (gt_rows, ncl), 1)
        slot = lane % 8
        jblk = 4 * (lane // 8) + slot
        t_gt = past + lax.broadcasted_iota(jnp.int32, (gt_rows, ncl), 0) % ds
        forced = (jblk == 0) | (jblk == t_gt // SEL_BLOCK)
        valid = jblk * SEL_BLOCK <= t_gt
        scores = jnp.where(slot < 4, jnp.where(forced, jnp.inf, jnp.where(valid, imp, NEG_INF)), NEG_INF)
        jf = jnp.where(slot < 4, jblk, 2 * nblk).astype(F32)
        picks = []
        for _ in range(k_top):
            mx = jnp.max(scores, axis=1, keepdims=True)
            idx = jnp.min(jnp.where(scores == mx, jf, float(2 * nblk)), axis=1, keepdims=True)
            scores = jnp.where(jf == idx, NEG_INF, scores)
            picks.append(jnp.concatenate([idx] * Q_PER_KV, axis=0))

        half = (lax.broadcasted_iota(jnp.int32, (1, PAGE_SIZE), 1) >= SEL_BLOCK).astype(F32)

        def page_step(p, carry, bias):
            m_old, l_old, acc = carry
            s = jnp.dot(wq, skt_s[p], preferred_element_type=F32)
            if bias is not None:
                s = s + bias
            jk = half + jnp.asarray(2 * p, F32)
            hit = jk == picks[0]
            for pk in picks[1:]:
                hit = hit | (jk == pk)
            s = jnp.where(hit, s, NEG_INF)
            m_new, alpha, l_new, pr = _softmax_lanes_online(s, m_old, l_old)
            pv = lax.dot_general(pr.astype(BF16), svt_s[p], nt, preferred_element_type=F32)
            return m_new, l_new, alpha * acc + pv

        carry = (jnp.full((rows, 1), M_FLOOR, F32), jnp.zeros((rows, 1), F32), jnp.zeros((rows, 256), F32))
        carry = lax.fori_loop(0, npages - NEAR_PAGES, lambda p, c: page_step(p, c, None), carry)
        for k in range(NEAR_PAGES):
            carry = page_step(npages - NEAR_PAGES + k, carry, bs_s[k])
        m_old, l_old, acc = carry
        lane_n = lax.broadcasted_iota(jnp.int32, (1, NEW_PAD), 1)
        ok_new = (lane_n < ds) & (lane_n <= tt)
        s = lax.dot_general(wq, new_ref[0, 0], nt, preferred_element_type=F32) + bnew_s[...]
        m_new, alpha, l_new, pr = _softmax_lanes_online(jnp.where(ok_new, s, NEG_INF), m_old, l_old)
        acc = alpha * acc + jnp.dot(pr.astype(BF16), new_ref[0, 1], preferred_element_type=F32)
        obr_s[1] = acc / l_new

        kwt = wint_ref[0, 0:256, :].astype(BF16)
        vwt = wint_ref[0, 256:512, :].astype(BF16)
        dist = tt + wb - lax.broadcasted_iota(jnp.int32, (1, wb), 1)
        s_w = jnp.where((dist >= 0) & (dist < WINDOW), jnp.dot(wq, kwt, preferred_element_type=F32) + bw_s[...], NEG_INF)
        s_n = jnp.where(ok_new, lax.dot_general(wq, new_ref[0, 2], nt, preferred_element_type=F32) + bnew_s[...], NEG_INF)
        m = jnp.maximum(jnp.max(s_w, axis=1, keepdims=True), jnp.max(s_n, axis=1, keepdims=True))
        e_w = jnp.exp(s_w - m)
        e_n = jnp.exp(s_n - m)
        den = jnp.sum(e_w, axis=1, keepdims=True) + jnp.sum(e_n, axis=1, keepdims=True)
        o_w = (lax.dot_general(e_w.astype(BF16), vwt, nt, preferred_element_type=F32)
               + jnp.dot(e_n.astype(BF16), new_ref[0, 3], preferred_element_type=F32))
        obr_s[2] = o_w / den

        merged = []
        for r in range(Q_PER_KV):
            tot = None
            for br in range(3):
                o_r = jnp.zeros((ds, 256), F32)
                g_r = jnp.zeros((ds, 256), F32)
                for g in range(N_KV):
                    base = r * gt_rows + g * ds
                    o_r = jnp.where(lane_g == g, obr_s[br, base:base + ds, :], o_r)
                    c = br * N_HEADS + g * Q_PER_KV + r
                    g_r = jnp.where(lane_g == g, gate_ref[0, :, c:c + 1], g_r)
                tot = g_r * o_r if tot is None else tot + g_r * o_r
            merged.append(tot)
        op = jnp.concatenate(merged, axis=1)
        y = jnp.dot((op * _silu(u_ref[0])).astype(BF16), wout_ref[...], preferred_element_type=F32)
        y_ref[0] = _rms(x1_ref[0] + y, gfin_ref[...])


def _attn_sample(page_table, pool_t, q64, newkv, win_t, gates, u_r, x1, perm, wkt_bd, wv_bd, tb_rows, w_out_r, gfin,
                 *, past, ds, wb):
    nb = page_table.shape[0]
    npages = past // PAGE_SIZE
    assert past % (CHUNK_PAGES * PAGE_SIZE) == 0 and npages >= NEAR_PAGES and past // SEL_BLOCK >= TOP_N
    npg = npages // PAGES_PER_STEP
    nchunk = npages // CHUNK_PAGES
    rows = Q_PER_KV * N_KV * ds
    ncl = past // CMP_BLOCK
    once = dict(pipeline_mode=pl.Buffered(1))
    full = lambda shp, **kw: pl.BlockSpec(shp, lambda b, p, pt: (0,) * len(shp), **kw)
    per_b = lambda shp: pl.BlockSpec((1,) + shp, lambda b, p, pt: (b,) + (0,) * len(shp))
    page_specs = [pl.BlockSpec((1, 1024, PAGE_SIZE),
                               functools.partial(lambda b, p, pt, k: (pt[b, p * PAGES_PER_STEP + k], 0, 0), k=k))
                  for k in range(PAGES_PER_STEP)]
    grid_spec = pltpu.PrefetchScalarGridSpec(
        num_scalar_prefetch=1, grid=(nb, npg),
        in_specs=[*page_specs, per_b((rows, 256)), per_b((4, NEW_PAD, 256)), per_b((512, wb)),
                  per_b((ds, GATE_PAD)), per_b((ds, 1024)), per_b((ds, D_MODEL)),
                  full((256, 256), **once), full((CMP_BLOCK, 256, 256), **once), full((CMP_BLOCK, 256, 256), **once),
                  full((rows, N_BUCKETS)), full((1024, D_MODEL), **once), full((1, D_MODEL))],
        out_specs=pl.BlockSpec((1, ds, D_MODEL), lambda b, p, pt: (b, 0, 0)),
        scratch_shapes=[pltpu.VMEM((CMP_BLOCK, CHUNK_PAGES * PAGE_SIZE // CMP_BLOCK, 512), F32),
                        pltpu.VMEM((nchunk, 256, 128), BF16), pltpu.VMEM((nchunk, 128, 256), BF16),
                        pltpu.VMEM((npages, 256, PAGE_SIZE), BF16), pltpu.VMEM((npages, 256, PAGE_SIZE), BF16),
                        pltpu.VMEM((rows, ncl), F32), pltpu.VMEM((NEAR_PAGES, rows, PAGE_SIZE), F32),
                        pltpu.VMEM((rows, NEW_PAD), F32), pltpu.VMEM((rows, wb), F32),
                        pltpu.VMEM((3, rows, 256), F32)])
    return pl.pallas_call(
        functools.partial(_attn_sample_kernel, past=past, ds=ds, wb=wb),
        grid_spec=grid_spec,
        out_shape=jax.ShapeDtypeStruct((nb, ds, D_MODEL), F32),
        compiler_params=_cparams("arbitrary", "arbitrary"),
        name="nsa_attn_sample",
    )(page_table, *([pool_t] * PAGES_PER_STEP), q64, newkv, win_t, gates, u_r, x1, perm, wkt_bd, wv_bd, tb_rows,
      w_out_r, gfin)


def _pair_permutation():
    m = jnp.arange(256)
    l, slot = m // 8, m % 8
    cb = jnp.where(slot < 4, 2 * slot, 2 * (slot - 4) + 1)
    tok = CMP_BLOCK * cb + l
    return (tok[:, None] == jnp.arange(256)[None, :]).astype(BF16)


def _block_diag4(w):
    eye = jnp.eye(N_KV, dtype=w.dtype)
    out = eye[:, None, :, None] * w[..., None, :, None, :]
    return out.reshape(*w.shape[:-2], N_KV * HEAD_DIM, N_KV * HEAD_DIM)


def kernel(x_prompt, x_sample, state_lru_h, state_conv, cache_nsa_kv, cache_win_kv, page_table, norm_g,
           final_norm_g, w_in_lru, conv_w, conv_b, w_gate_a, b_gate_a, w_gate_x, b_gate_x, lru_lambda,
           w_out_lru, w_in_nsa, w_cmp, w_out_nsa, rel_bias):
    bp, s_len, _ = x_prompt.shape
    nb, ds, _ = x_sample.shape
    assert bp == 1
    row = lambda v: v.reshape(1, -1)

    lru_w = (row(norm_g[0]), w_in_lru[0].astype(BF16), conv_w[0], row(conv_b[0]), w_gate_a[0].astype(BF16),
             row(b_gate_a[0]), w_gate_x[0].astype(BF16), row(b_gate_x[0]), row(lru_lambda[0]),
             w_out_lru[0].astype(BF16))
    x1p, hp, tailp = _lru_prompt(x_prompt[0], *lru_w)
    xs_tm = x_sample.transpose(1, 0, 2).reshape(ds * nb, D_MODEL)
    buf_tm = state_conv[0].transpose(1, 0, 2).reshape((CONV_W - 1) * nb, D_MODEL)
    x1s_tm, hs_last, tails = _lru_sample(xs_tm, buf_tm, state_lru_h[0], *lru_w, nb=nb, ds=ds)

    prompt_lru_h = hp.reshape(1, 1, D_MODEL)
    prompt_conv = tailp[8 - (CONV_W - 1):].reshape(1, 1, CONV_W - 1, D_MODEL)
    sample_lru_h = hs_last.reshape(1, nb, D_MODEL)
    sample_conv = tails.reshape(CONV_W - 1, nb, D_MODEL).transpose(1, 0, 2)[None]

    w = w_in_nsa[0]
    wq, wkv = w[:, :1024], w[:, 1024:1024 + KV_WIDTH]
    wg, wu = w[:, 1024 + KV_WIDTH:1024 + KV_WIDTH + 3 * N_HEADS], w[:, 1024 + KV_WIDTH + 3 * N_HEADS:]
    wg_pad = jnp.pad(wg, ((0, 0), (0, GATE_PAD - 3 * N_HEADS)))
    w_nat_p = jnp.concatenate([wkv[:, 0:768], wkv[:, 1024:1280], wu], axis=1).astype(BF16)
    w_t_p = jnp.concatenate([wq.T, wkv.T, wg_pad.T], axis=0).astype(BF16)
    wk_bd = _block_diag4(w_cmp[0, 0]).astype(BF16)
    wvt_bd = jnp.swapaxes(_block_diag4(w_cmp[0, 1]), -1, -2).astype(BF16)
    w_out = w_out_nsa[0].astype(BF16)
    g1 = row(norm_g[1])
    gfin = row(final_norm_g)

    cmp_kv, selk, wink, u_p, qt, kvcst, kvwt, selvt, winvt, gt = _proj_prompt(x1p, g1, w_nat_p, w_t_p)
    kck, kcvt = _compress_prompt(cmp_kv, wk_bd, wvt_bd)
    y_prompt = _attn_prompt(rel_bias, qt, kck, kcvt, selk, selvt, wink, winvt, gt, u_p, x1p, w_out, gfin)
    y_prompt = y_prompt[None]
    prompt_kv_rows = kvcst.reshape(4, N_KV, HEAD_DIM, s_len).transpose(3, 0, 1, 2)[None, None]
    wbp = min(WINDOW, s_len)
    prompt_win_kv = kvwt[:, s_len - wbp:].reshape(2, N_KV, HEAD_DIM, wbp).transpose(3, 0, 1, 2)[None, None]

    n_phys, n_layers = cache_nsa_kv.shape[:2]
    past = page_table.shape[1] * PAGE_SIZE
    wb = cache_win_kv.shape[2]
    to_rgd = lambda m: m.reshape(m.shape[0], N_KV, Q_PER_KV, HEAD_DIM).transpose(0, 2, 1, 3).reshape(m.shape[0], 1024)
    w_nat_s = jnp.concatenate([wkv, to_rgd(wq), to_rgd(wu), wg_pad], axis=1).astype(BF16)
    w_out_r = to_rgd(w_out_nsa[0].T).T.astype(BF16)
    x1s = x1s_tm.reshape(ds, nb, D_MODEL).transpose(1, 0, 2).reshape(nb * ds, D_MODEL)
    kvcs_s, kvw_s, q_s, u_s, gate_s = _proj_sample(x1s, g1, w_nat_s)
    q64 = q_s.reshape(nb, ds, Q_PER_KV, 1, 256).transpose(0, 2, 3, 1, 4)
    q64 = jnp.broadcast_to(q64, (nb, Q_PER_KV, N_KV, ds, 256)).reshape(nb, Q_PER_KV * N_KV * ds, 256)
    new4 = jnp.stack([kvcs_s[:, 512:768], kvcs_s[:, 768:1024], kvw_s[:, 0:256], kvw_s[:, 256:512]], axis=0)
    new4 = new4.reshape(4, nb, ds, 256).transpose(1, 0, 2, 3).astype(BF16)
    newkv = jnp.pad(new4, ((0, 0), (0, 0), (0, NEW_PAD - ds), (0, 0)))
    pool_t = cache_nsa_kv.transpose(0, 1, 3, 4, 5, 2).reshape(n_phys * n_layers, 1024, PAGE_SIZE)
    win_t = cache_win_kv[0].transpose(0, 2, 3, 4, 1).reshape(nb, 512, wb)
    head_of_row = (jnp.arange(Q_PER_KV * N_KV * ds) // ds % N_KV) * Q_PER_KV + jnp.arange(Q_PER_KV * N_KV * ds) // (N_KV * ds)
    tb_rows = rel_bias.T[head_of_row]
    y_sample = _attn_sample(page_table * n_layers, pool_t, q64, newkv, win_t, gate_s.reshape(nb, ds, GATE_PAD),
                            u_s.reshape(nb, ds, 1024), x1s.reshape(nb, ds, D_MODEL), _pair_permutation(),
                            jnp.swapaxes(wk_bd, -1, -2), jnp.swapaxes(wvt_bd, -1, -2), tb_rows, w_out_r, gfin,
                            past=past, ds=ds, wb=wb)
    sample_kv_rows = kvcs_s.reshape(nb, 1, ds, 4, N_KV, HEAD_DIM)
    new_t = kvw_s.reshape(nb, ds, 512).transpose(0, 2, 1)
    win_all = jnp.concatenate([win_t, new_t], axis=2)[:, :, ds:]
    sample_win_kv = win_all.reshape(nb, 2, N_KV, HEAD_DIM, wb).transpose(0, 4, 1, 2, 3)[None]

    return (y_prompt, y_sample, prompt_lru_h, prompt_conv, prompt_kv_rows, prompt_win_kv,
            sample_lru_h, sample_conv, sample_kv_rows, sample_win_kv)
```

```python
import functools
import math

import jax
import jax.numpy as jnp
from jax import lax
from jax.experimental import pallas as pl
from jax.experimental.pallas import tpu as pltpu

F32 = jnp.float32
BF16 = jnp.bfloat16

D_MODEL = 1024
HEAD_DIM = 64
N_HEADS = 16
N_KV = 4
Q_PER_KV = 4
CMP_BLOCK = 32
SEL_BLOCK = 64
TOP_N = 16
WINDOW = 512
PAGE_SIZE = 128
N_BUCKETS = 32
MAX_DISTANCE = 1024
N_LRU_BLOCKS = 8
LRU_BLOCK = 128
CONV_W = 4
LRU_C = 8.0
RMS_EPS = 1e-6

V7X_VMEM_BYTES = 64 * 1024 * 1024
VMEM_LIMIT = V7X_VMEM_BYTES - 8 * 1024 * 1024


def _bucket_lower_bounds():
    max_exact = N_BUCKETS // 2
    lows = list(range(1, max_exact + 1))
    for k in range(1, N_BUCKETS - max_exact):
        n = max_exact
        while n ** 8 < 2 ** (32 + 3 * k):
            n += 1
        lows.append(n)
    return lows


_BUCKET_LOWS = _bucket_lower_bounds()


def _bias_from_dist(dist, table_col):
    val = jnp.full(dist.shape, table_col(0), F32)
    for b, lo in enumerate(_BUCKET_LOWS, start=1):
        val = jnp.where(dist >= lo, table_col(b), val)
    return val


def _rms(x, g):
    ms = jnp.mean(x * x, axis=-1, keepdims=True)
    return x * lax.rsqrt(ms + RMS_EPS) * g


def _sigmoid(x):
    return 1.0 / (1.0 + jnp.exp(-x))


def _silu(x):
    return x * _sigmoid(x)


def _softplus(x):
    return jnp.maximum(x, 0.0) + jnp.log1p(jnp.exp(-jnp.abs(x)))


def _expm1(x):
    u = jnp.exp(x)
    um1 = u - 1.0
    y = um1 * x / jnp.log(u)
    y = jnp.where(u == 1.0, x, y)
    return jnp.where(um1 == -1.0, -1.0, y)


def _cparams(*sem):
    return pltpu.CompilerParams(dimension_semantics=sem, vmem_limit_bytes=VMEM_LIMIT)


def _lru_gate_terms(conv, wa_ref, ba, wx_ref, bx, lam):
    cb = conv.astype(BF16)
    r_parts, i_parts = [], []
    for n in range(N_LRU_BLOCKS):
        c = cb[:, n * LRU_BLOCK:(n + 1) * LRU_BLOCK]
        r_parts.append(jnp.dot(c, wa_ref[n], preferred_element_type=F32))
        i_parts.append(jnp.dot(c, wx_ref[n], preferred_element_type=F32))
    r = _sigmoid(jnp.concatenate(r_parts, axis=1) + ba)
    i = _sigmoid(jnp.concatenate(i_parts, axis=1) + bx)
    log_a = -LRU_C * r * _softplus(-lam)
    a = jnp.exp(log_a)
    u = jnp.sqrt(-_expm1(2.0 * log_a)) * i * conv
    return a, u


def _lru_prompt_kernel(x_ref, g_ref, win_ref, cw_ref, cb_ref, wa_ref, ba_ref, wx_ref, bx_ref,
                       lam_ref, wout_ref, x1_ref, hlast_ref, tail_ref,
                       xb_ext, a_s, u_s, h_c, *, tr):
    i = pl.program_id(0)

    @pl.when(i == 0)
    def _():
        xb_ext[0:8, :] = jnp.zeros((8, D_MODEL), F32)
        h_c[...] = jnp.zeros_like(h_c)

    x = x_ref[...]
    xn = _rms(x, g_ref[...])
    z = jnp.dot(xn.astype(BF16), win_ref[...], preferred_element_type=F32)
    xb = z[:, :D_MODEL]
    gate = z[:, D_MODEL:]
    xb_ext[8:8 + tr, :] = xb
    conv = cb_ref[...]
    for j in range(CONV_W):
        conv = conv + cw_ref[j:j + 1, :] * xb_ext[8 - (CONV_W - 1) + j:8 - (CONV_W - 1) + j + tr, :]
    a, u = _lru_gate_terms(conv, wa_ref, ba_ref[...], wx_ref, bx_ref[...], lam_ref[...])
    a_s[...] = a
    u_s[...] = u

    row = lax.broadcasted_iota(jnp.int32, (8, D_MODEL), 0)

    def body(k, h):
        base = pl.multiple_of(k * 8, 8)
        aa = a_s[pl.ds(base, 8), :]
        uu = u_s[pl.ds(base, 8), :]
        for s in (1, 2, 4):
            keep = row >= s
            us = jnp.where(keep, pltpu.roll(uu, s, axis=0), 0.0)
            as_ = jnp.where(keep, pltpu.roll(aa, s, axis=0), 1.0)
            uu = uu + aa * us
            aa = aa * as_
        hb = aa * h + uu
        u_s[pl.ds(base, 8), :] = hb
        return hb[7:8, :]

    h_last = lax.fori_loop(0, tr // 8, body, h_c[...])
    h_c[...] = h_last
    hlast_ref[...] = h_last
    tail_ref[...] = xb[tr - 8:, :]
    xb_ext[0:8, :] = xb[tr - 8:, :]
    y = u_s[...] * _silu(gate)
    out = jnp.dot(y.astype(BF16), wout_ref[...], preferred_element_type=F32)
    x1_ref[...] = x + out


def _lru_prompt(x, g, w_in, cw, cb, wa, ba, wx, bx, lam, w_out, tr=512):
    s = x.shape[0]
    tr = min(tr, s)
    assert s % tr == 0 and tr % 8 == 0
    full = lambda shp: pl.BlockSpec(shp, lambda i: (0,) * len(shp))
    return pl.pallas_call(
        functools.partial(_lru_prompt_kernel, tr=tr),
        grid=(s // tr,),
        in_specs=[pl.BlockSpec((tr, D_MODEL), lambda i: (i, 0)), full((1, D_MODEL)),
                  full((D_MODEL, 2 * D_MODEL)), full((CONV_W, D_MODEL)), full((1, D_MODEL)),
                  full((N_LRU_BLOCKS, LRU_BLOCK, LRU_BLOCK)), full((1, D_MODEL)),
                  full((N_LRU_BLOCKS, LRU_BLOCK, LRU_BLOCK)), full((1, D_MODEL)),
                  full((1, D_MODEL)), full((D_MODEL, D_MODEL))],
        out_specs=[pl.BlockSpec((tr, D_MODEL), lambda i: (i, 0)), full((1, D_MODEL)), full((8, D_MODEL))],
        out_shape=[jax.ShapeDtypeStruct((s, D_MODEL), F32), jax.ShapeDtypeStruct((1, D_MODEL), F32),
                   jax.ShapeDtypeStruct((8, D_MODEL), F32)],
        scratch_shapes=[pltpu.VMEM((tr + 8, D_MODEL), F32), pltpu.VMEM((tr, D_MODEL), F32),
                        pltpu.VMEM((tr, D_MODEL), F32), pltpu.VMEM((1, D_MODEL), F32)],
        compiler_params=_cparams("arbitrary"),
        name="lru_prompt",
    )(x, g, w_in, cw, cb, wa, ba, wx, bx, lam, w_out)


def _lru_sample_kernel(x_ref, buf_ref, h0_ref, g_ref, win_ref, cw_ref, cb_ref, wa_ref, ba_ref, wx_ref,
                       bx_ref, lam_ref, wout_ref, x1_ref, hlast_ref, tail_ref, xp_s, *, nb, ds):
    x = x_ref[...]
    xn = _rms(x, g_ref[...])
    z = jnp.dot(xn.astype(BF16), win_ref[...], preferred_element_type=F32)
    xb = z[:, :D_MODEL]
    gate = z[:, D_MODEL:]
    nbuf = (CONV_W - 1) * nb
    xp_s[0:nbuf, :] = buf_ref[...]
    xp_s[nbuf:, :] = xb
    conv = cb_ref[...]
    for j in range(CONV_W):
        conv = conv + cw_ref[j:j + 1, :] * xp_s[j * nb:j * nb + ds * nb, :]
    a, u = _lru_gate_terms(conv, wa_ref, ba_ref[...], wx_ref, bx_ref[...], lam_ref[...])
    h = h0_ref[...]
    hs = []
    for t in range(ds):
        h = a[t * nb:(t + 1) * nb, :] * h + u[t * nb:(t + 1) * nb, :]
        hs.append(h)
    hlast_ref[...] = h
    tail_ref[...] = xp_s[ds * nb:, :]
    y = jnp.concatenate(hs, axis=0) * _silu(gate)
    out = jnp.dot(y.astype(BF16), wout_ref[...], preferred_element_type=F32)
    x1_ref[...] = x + out


def _lru_sample(x_tm, buf_tm, h0, g, w_in, cw, cb, wa, ba, wx, bx, lam, w_out, nb, ds):
    rows = nb * ds
    nbuf = (CONV_W - 1) * nb
    return pl.pallas_call(
        functools.partial(_lru_sample_kernel, nb=nb, ds=ds),
        out_shape=[jax.ShapeDtypeStruct((rows, D_MODEL), F32), jax.ShapeDtypeStruct((nb, D_MODEL), F32),
                   jax.ShapeDtypeStruct((nbuf, D_MODEL), F32)],
        scratch_shapes=[pltpu.VMEM((nbuf + rows, D_MODEL), F32)],
        compiler_params=pltpu.CompilerParams(vmem_limit_bytes=VMEM_LIMIT),
        name="lru_sample",
    )(x_tm, buf_tm, h0, g, w_in, cw, cb, wa, ba, wx, bx, lam, w_out)


Q_BLOCK = 128
LANES_G = Q_PER_KV * Q_BLOCK
KEY_TILE = 512
NEAR_PAD = KEY_TILE
NEAR_ROWS = NEAR_PAD + MAX_DISTANCE + KEY_TILE
NEG_INF = float("-inf")
LOG2E = math.log2(math.e)
M_FLOOR = -1e30

KV_WIDTH = 3 * 2 * N_KV * HEAD_DIM
GATE_PAD = 128


def _proj_prompt_kernel(x_ref, g_ref, wn_ref, wt_ref, cmp_ref, selk_ref, wink_ref, u_ref,
                        qt_ref, kvcst_ref, kvwt_ref, selvt_ref, winvt_ref, gt_ref, *, tr):
    xn = _rms(x_ref[...], g_ref[...]).astype(BF16)
    z = jnp.dot(xn, wn_ref[...], preferred_element_type=F32)
    cmp_ref[...] = z[:, 0:512].astype(BF16)
    for p in range(2):
        selk_ref[p] = z[:, 512 + 128 * p:640 + 128 * p].astype(BF16)
        wink_ref[p] = z[:, 768 + 128 * p:896 + 128 * p].astype(BF16)
    u_ref[...] = z[:, 1024:2048]
    zt = lax.dot_general(wt_ref[...], xn, (((1,), (1,)), ((), ())), preferred_element_type=F32)
    qt_ref[...] = (zt[0:1024] * (HEAD_DIM ** -0.5 * LOG2E)).astype(BF16).reshape(N_KV, Q_PER_KV * HEAD_DIM, tr)
    kvt = zt[1024:1024 + KV_WIDTH]
    kvcst_ref[...] = kvt[0:1024]
    kvwt_ref[...] = kvt[1024:1536]
    selvt_ref[...] = kvt[768:1024].astype(BF16).reshape(N_KV, HEAD_DIM, tr)
    winvt_ref[...] = kvt[1280:1536].astype(BF16).reshape(N_KV, HEAD_DIM, tr)
    gt_ref[...] = _sigmoid(zt[1024 + KV_WIDTH:1024 + KV_WIDTH + GATE_PAD])


def _proj_prompt(x1, g, w_nat, w_t, tr=512):
    s = x1.shape[0]
    tr = min(tr, s)
    nn, nt = w_nat.shape[1], w_t.shape[0]
    full = lambda shp: pl.BlockSpec(shp, lambda i: (0,) * len(shp))
    sds = jax.ShapeDtypeStruct
    return pl.pallas_call(
        functools.partial(_proj_prompt_kernel, tr=tr),
        grid=(s // tr,),
        in_specs=[pl.BlockSpec((tr, D_MODEL), lambda i: (i, 0)), full((1, D_MODEL)), full((D_MODEL, nn)),
                  full((nt, D_MODEL))],
        out_specs=[pl.BlockSpec((tr, 512), lambda i: (i, 0)),
                   pl.BlockSpec((2, tr, 128), lambda i: (0, i, 0)), pl.BlockSpec((2, tr, 128), lambda i: (0, i, 0)),
                   pl.BlockSpec((tr, 1024), lambda i: (i, 0)),
                   pl.BlockSpec((N_KV, 256, tr), lambda i: (0, 0, i)),
                   pl.BlockSpec((1024, tr), lambda i: (0, i)), pl.BlockSpec((512, tr), lambda i: (0, i)),
                   pl.BlockSpec((N_KV, HEAD_DIM, tr), lambda i: (0, 0, i)),
                   pl.BlockSpec((N_KV, HEAD_DIM, tr), lambda i: (0, 0, i)),
                   pl.BlockSpec((GATE_PAD, tr), lambda i: (0, i))],
        out_shape=[sds((s, 512), BF16), sds((2, s, 128), BF16), sds((2, s, 128), BF16),
                   sds((s, 1024), F32), sds((N_KV, 256, s), BF16), sds((1024, s), F32), sds((512, s), F32),
                   sds((N_KV, HEAD_DIM, s), BF16), sds((N_KV, HEAD_DIM, s), BF16), sds((GATE_PAD, s), F32)],
        compiler_params=_cparams("arbitrary"),
        name="nsa_proj_prompt",
    )(x1, g, w_nat, w_t)


def _compress_prompt_kernel(xe_ref, xo_ref, wk_ref, wvt_ref, kck_ref, kcvt_ref, acck, accv, *, tl, nh):
    step = pl.program_id(0)

    @pl.when(step == 0)
    def _():
        acck[...] = jnp.zeros_like(acck)
        accv[...] = jnp.zeros_like(accv)

    for half, x_ref in enumerate((xe_ref, xo_ref)):
        for l in range(tl):
            xk = x_ref[:, l * 512:l * 512 + 256]
            xv = x_ref[:, l * 512 + 256:l * 512 + 512]
            acck[half * nh:(half + 1) * nh, :] += jnp.dot(xk, wk_ref[l], preferred_element_type=F32)
            accv[:, half * nh:(half + 1) * nh] += lax.dot_general(
                wvt_ref[l], xv, (((1,), (1,)), ((), ())), preferred_element_type=F32)

    @pl.when(step == pl.num_programs(0) - 1)
    def _():
        for p in range(2):
            kck_ref[p] = acck[:, 128 * p:128 * (p + 1)].astype(BF16)
        kcvt_ref[...] = accv[...].astype(BF16).reshape(N_KV, HEAD_DIM, 2 * nh)


def _compress_prompt(cmp_kv, wk_bd, wvt_bd, tl=4):
    s = cmp_kv.shape[0]
    nh = s // (2 * CMP_BLOCK)
    x2 = cmp_kv.reshape(nh, 2 * CMP_BLOCK * 512)
    nsteps = CMP_BLOCK // tl
    return pl.pallas_call(
        functools.partial(_compress_prompt_kernel, tl=tl, nh=nh),
        grid=(nsteps,),
        in_specs=[pl.BlockSpec((nh, tl * 512), lambda i: (0, i)),
                  pl.BlockSpec((nh, tl * 512), lambda i: (0, nsteps + i)),
                  pl.BlockSpec((tl, 256, 256), lambda i: (i, 0, 0)),
                  pl.BlockSpec((tl, 256, 256), lambda i: (i, 0, 0))],
        out_specs=[pl.BlockSpec((2, 2 * nh, 128), lambda i: (0, 0, 0)),
                   pl.BlockSpec((N_KV, HEAD_DIM, 2 * nh), lambda i: (0, 0, 0))],
        out_shape=[jax.ShapeDtypeStruct((2, 2 * nh, 128), BF16),
                   jax.ShapeDtypeStruct((N_KV, HEAD_DIM, 2 * nh), BF16)],
        scratch_shapes=[pltpu.VMEM((2 * nh, 256), F32), pltpu.VMEM((256, 2 * nh), F32)],
        compiler_params=_cparams("arbitrary"),
        name="nsa_compress_prompt",
    )(x2, x2, wk_bd, wvt_bd)


def _softmax_rows(s):
    m = jnp.max(s, axis=0, keepdims=True)
    m = jnp.where(m == NEG_INF, 0.0, m)
    e = jnp.exp2(s - m)
    den = jnp.sum(e, axis=0, keepdims=True)
    return e / jnp.where(den > 0, den, 1.0)


def _attn_prompt_kernel(bias_ref, qt_ref, kck_ref, kcvt_ref, selk_ref, selvt_ref,
                        wk0, wk1, wk2, wk3, wk4, wv0, wv1, wv2, wv3, wv4,
                        gt_ref, u_ref, x1_ref, wout_ref, gfin_ref, y_ref,
                        nb_s, wq_s, pen_s, sc_s, obr_s, opt_s, *, s_len):
    i = pl.program_id(0)
    q0 = i * Q_BLOCK
    nc = s_len // CMP_BLOCK
    nch = nc // 2
    nblk = s_len // SEL_BLOCK
    k_top = min(TOP_N, nblk)
    wrows = min(32, nch)
    wk_refs = (wk0, wk1, wk2, wk3, wk4)
    wv_refs = (wv0, wv1, wv2, wv3, wv4)
    n_win = len(wk_refs)

    def table(h):
        return lambda b: (bias_ref[b, h] - bias_ref[N_BUCKETS - 1, h]) * LOG2E

    @pl.when(i == 0)
    def _():
        wq_s[...] = jnp.zeros_like(wq_s)
        rho = lax.broadcasted_iota(jnp.int32, (NEAR_ROWS, Q_BLOCK), 0)
        a = lax.broadcasted_iota(jnp.int32, (NEAR_ROWS, Q_BLOCK), 1)
        dist = a + (MAX_DISTANCE + NEAR_PAD) - rho
        for h in range(N_HEADS):
            nb_s[h] = _bias_from_dist(dist, table(h))

    for g in range(N_KV):
        off = (g % 2) * HEAD_DIM
        for r in range(Q_PER_KV):
            wq_s[g, off:off + HEAD_DIM, r * Q_BLOCK:(r + 1) * Q_BLOCK] = qt_ref[g, r * HEAD_DIM:(r + 1) * HEAD_DIM, :]

    lane_q = lax.broadcasted_iota(jnp.int32, (1, LANES_G), 1) % Q_BLOCK
    t_lane = q0 + lane_q
    blk = lax.broadcasted_iota(jnp.int32, (nblk, Q_BLOCK), 0)
    tq = lax.broadcasted_iota(jnp.int32, (nblk, Q_BLOCK), 1) + q0
    forced = (blk == 0) | (blk == tq // SEL_BLOCK)
    valid = blk * SEL_BLOCK <= tq
    blk_f = blk.astype(F32)

    def group_body(g, carry):
        wq = wq_s[g]
        pair = g // 2

        sc_s[...] = jnp.dot(kck_ref[pair], wq, preferred_element_type=F32)
        w0 = jnp.clip(((2 * i - 16) // 8) * 8, 0, nch - wrows)
        w0 = pl.multiple_of(w0, 8)
        jj = lax.broadcasted_iota(jnp.int32, (wrows, Q_BLOCK), 0) + w0
        aq = lax.broadcasted_iota(jnp.int32, (wrows, Q_BLOCK), 1) + q0
        for half in range(2):
            dist = aq - SEL_BLOCK * jj - (CMP_BLOCK * half + CMP_BLOCK - 1)
            rows = pl.ds(pl.multiple_of(half * nch + w0, 8), wrows)
            for r in range(Q_PER_KV):
                sc_s[rows, r * Q_BLOCK:(r + 1) * Q_BLOCK] += _bias_from_dist(dist, table(g * Q_PER_KV + r))
        rho = lax.broadcasted_iota(jnp.int32, (nc, 1), 0)
        ends = jnp.where(rho >= nch, SEL_BLOCK * (rho - nch) + 2 * CMP_BLOCK - 1, SEL_BLOCK * rho + CMP_BLOCK - 1)
        p_c = _softmax_rows(jnp.where(ends <= t_lane, sc_s[...], NEG_INF))
        obr_s[0, g] = jnp.dot(kcvt_ref[g], p_c.astype(BF16), preferred_element_type=F32)

        psum = p_c[:, 0:Q_BLOCK]
        for r in range(1, Q_PER_KV):
            psum = psum + p_c[:, r * Q_BLOCK:(r + 1) * Q_BLOCK]
        imp = psum[0:nch] + psum[nch:nc]
        pen_s[g, :, 0:Q_BLOCK] = jnp.where(forced, jnp.inf, jnp.where(valid, imp, NEG_INF))

        kw = jnp.concatenate([wk_refs[k][pair] for k in range(n_win)], axis=0)
        s = jnp.dot(kw, wq, preferred_element_type=F32)
        nw = n_win * Q_BLOCK
        r_w = NEAR_PAD + MAX_DISTANCE - WINDOW
        bias = jnp.concatenate([nb_s[g * Q_PER_KV + r, r_w:r_w + nw, :] for r in range(Q_PER_KV)], axis=1)
        loc = lax.broadcasted_iota(jnp.int32, (nw, 1), 0)
        dist = lane_q + WINDOW - loc
        ok = (dist >= 0) & (dist < WINDOW) & (q0 - WINDOW + loc >= 0)
        p_w = _softmax_rows(jnp.where(ok, s + bias, NEG_INF))
        vw = jnp.concatenate([wv_refs[k][g] for k in range(n_win)], axis=1)
        obr_s[2, g] = jnp.dot(vw, p_w.astype(BF16), preferred_element_type=F32)
        return carry

    lax.fori_loop(0, N_KV, group_body, 0)

    def pick(_, c):
        for g in range(N_KV):
            sc = pen_s[g, :, 0:Q_BLOCK]
            m = jnp.max(sc, axis=0, keepdims=True)
            idx = jnp.min(jnp.where(sc == m, blk_f, float(nblk)), axis=0, keepdims=True)
            pen_s[g, :, 0:Q_BLOCK] = jnp.where(blk_f == idx, NEG_INF, sc)
        return c

    lax.fori_loop(0, k_top, pick, 0)
    for g in range(N_KV):
        picked = (pen_s[g, :, 0:Q_BLOCK] == NEG_INF) & (forced | valid)
        pen = jnp.where(picked, 0.0, NEG_INF)
        pen_s[g] = jnp.concatenate([pen] * Q_PER_KV, axis=1)

    n_tiles = (q0 + Q_BLOCK - 1) // KEY_TILE + 1
    n_far = jnp.maximum(q0 - MAX_DISTANCE, 0) // KEY_TILE
    nbt = KEY_TILE // SEL_BLOCK

    def tile(tt, state, near):
        k0 = pl.multiple_of(tt * KEY_TILE, KEY_TILE)
        if near:
            r0 = pl.multiple_of(jnp.maximum(k0 - q0 + MAX_DISTANCE + NEAR_PAD, 0), Q_BLOCK)
            causal = k0 + lax.broadcasted_iota(jnp.int32, (KEY_TILE, 1), 0) <= t_lane
        out = []
        logits = [jnp.dot(selk_ref[g // 2, pl.ds(k0, KEY_TILE), :], wq_s[g], preferred_element_type=F32)
                  for g in range(N_KV)]
        for g in range(N_KV):
            m_old, l_old, acc = state[g]
            s = logits[g]
            if near:
                bias = jnp.concatenate(
                    [nb_s[g * Q_PER_KV + r, pl.ds(r0, KEY_TILE), :] for r in range(Q_PER_KV)], axis=1)
                s = jnp.where(causal, s + bias, NEG_INF)
            pen_t = pen_s[g, pl.ds(pl.multiple_of(tt * nbt, nbt), nbt), :]
            s = (s.reshape(nbt, SEL_BLOCK, LANES_G) + pen_t[:, None, :]).reshape(KEY_TILE, LANES_G)
            m_new = jnp.maximum(m_old, jnp.max(s, axis=0, keepdims=True))
            alpha = jnp.exp2(m_old - m_new)
            p = jnp.exp2(s - m_new)
            l_new = alpha * l_old + jnp.sum(p, axis=0, keepdims=True)
            pv = jnp.dot(selvt_ref[g, :, pl.ds(k0, KEY_TILE)], p.astype(BF16), preferred_element_type=F32)
            out.append((m_new, l_new, alpha * acc + pv))
        return tuple(out)

    init = tuple((jnp.full((1, LANES_G), M_FLOOR, F32), jnp.zeros((1, LANES_G), F32),
                  jnp.zeros((HEAD_DIM, LANES_G), F32)) for _ in range(N_KV))
    far_done = lax.fori_loop(0, n_far, lambda tt, c: tile(tt, c, False), init)
    sel_done = lax.fori_loop(n_far, n_tiles, lambda tt, c: tile(tt, c, True), far_done)
    for g in range(N_KV):
        obr_s[1, g] = sel_done[g][2] / sel_done[g][1]

    for g in range(N_KV):
        for r in range(Q_PER_KV):
            h = g * Q_PER_KV + r
            sl = slice(r * Q_BLOCK, (r + 1) * Q_BLOCK)
            merged = (gt_ref[h:h + 1, :] * obr_s[0, g, :, sl] + gt_ref[N_HEADS + h:N_HEADS + h + 1, :] * obr_s[1, g, :, sl]
                      + gt_ref[2 * N_HEADS + h:2 * N_HEADS + h + 1, :] * obr_s[2, g, :, sl])
            opt_s[h * HEAD_DIM:(h + 1) * HEAD_DIM, :] = merged
    op = opt_s[...].T
    y = jnp.dot((op * _silu(u_ref[...])).astype(BF16), wout_ref[...], preferred_element_type=F32)
    y_ref[...] = _rms(x1_ref[...] + y, gfin_ref[...])


def _attn_prompt(rel_bias, qt, kck, kcvt, selk, selvt, wink, winvt, gt, u, x1, w_out, gfin):
    s = x1.shape[0]
    nq = s // Q_BLOCK
    nc = s // CMP_BLOCK
    nblk = s // SEL_BLOCK
    assert s % (2 * CMP_BLOCK * 128) == 0 and nblk >= TOP_N
    n_win = WINDOW // Q_BLOCK + 1
    full = lambda shp, **kw: pl.BlockSpec(shp, lambda i: (0,) * len(shp), **kw)
    once = dict(pipeline_mode=pl.Buffered(1))
    win_k_specs = [pl.BlockSpec((2, Q_BLOCK, 128), functools.partial(lambda i, k: (0, jnp.maximum(i - (n_win - 1) + k, 0), 0), k=k))
                   for k in range(n_win)]
    win_v_specs = [pl.BlockSpec((N_KV, HEAD_DIM, Q_BLOCK), functools.partial(lambda i, k: (0, 0, jnp.maximum(i - (n_win - 1) + k, 0)), k=k))
                   for k in range(n_win)]
    return pl.pallas_call(
        functools.partial(_attn_prompt_kernel, s_len=s),
        grid=(nq,),
        in_specs=[pl.BlockSpec(memory_space=pltpu.SMEM),
                  pl.BlockSpec((N_KV, 256, Q_BLOCK), lambda i: (0, 0, i)),
                  full((2, nc, 128), **once), full((N_KV, HEAD_DIM, nc), **once),
                  full((2, s, 128), **once), full((N_KV, HEAD_DIM, s), **once),
                  *win_k_specs, *win_v_specs,
                  pl.BlockSpec((GATE_PAD, Q_BLOCK), lambda i: (0, i)),
                  pl.BlockSpec((Q_BLOCK, 1024), lambda i: (i, 0)),
                  pl.BlockSpec((Q_BLOCK, D_MODEL), lambda i: (i, 0)),
                  full((1024, D_MODEL), **once), full((1, D_MODEL))],
        out_specs=pl.BlockSpec((Q_BLOCK, D_MODEL), lambda i: (i, 0)),
        out_shape=jax.ShapeDtypeStruct((s, D_MODEL), F32),
        scratch_shapes=[pltpu.VMEM((N_HEADS, NEAR_ROWS, Q_BLOCK), F32),
                        pltpu.VMEM((N_KV, 128, LANES_G), BF16),
                        pltpu.VMEM((N_KV, nblk, LANES_G), F32),
                        pltpu.VMEM((nc, LANES_G), F32),
                        pltpu.VMEM((3, N_KV, HEAD_DIM, LANES_G), F32),
                        pltpu.VMEM((N_HEADS * HEAD_DIM, Q_BLOCK), F32)],
        compiler_params=_cparams("arbitrary"),
        name="nsa_attn_prompt",
    )(rel_bias, qt, kck, kcvt, selk, selvt, *([wink] * n_win), *([winvt] * n_win), gt, u, x1, w_out, gfin)


PAGES_PER_STEP = 4
CHUNK_PAGES = 32
STEPS_PER_CHUNK = CHUNK_PAGES // PAGES_PER_STEP
NEAR_PAGES = MAX_DISTANCE // PAGE_SIZE
NEW_PAD = 128


def _proj_sample_kernel(x_ref, g_ref, wn_ref, kvcs_ref, kvw_ref, q_ref, u_ref, gate_ref):
    xn = _rms(x_ref[...], g_ref[...]).astype(BF16)
    z = jnp.dot(xn, wn_ref[...], preferred_element_type=F32)
    kvcs_ref[...] = z[:, 0:1024]
    kvw_ref[...] = z[:, 1024:1536]
    q_ref[...] = (z[:, 1536:2560] * (HEAD_DIM ** -0.5)).astype(BF16)
    u_ref[...] = z[:, 2560:3584]
    gate_ref[...] = _sigmoid(z[:, 3584:3584 + GATE_PAD])


def _proj_sample(x1, g, w_nat):
    rows = x1.shape[0]
    sds = jax.ShapeDtypeStruct
    return pl.pallas_call(
        _proj_sample_kernel,
        out_shape=[sds((rows, 1024), F32), sds((rows, 512), F32), sds((rows, 1024), BF16),
                   sds((rows, 1024), F32), sds((rows, GATE_PAD), F32)],
        compiler_params=pltpu.CompilerParams(vmem_limit_bytes=VMEM_LIMIT),
        name="nsa_proj_sample",
    )(x1, g, w_nat)


def _softmax_lanes_online(s, m_old, l_old):
    m_new = jnp.maximum(m_old, jnp.max(s, axis=1, keepdims=True))
    alpha = jnp.exp(m_old - m_new)
    p = jnp.exp(s - m_new)
    return m_new, alpha, alpha * l_old + jnp.sum(p, axis=1, keepdims=True), p


def _attn_sample_kernel(pt_ref, pg0, pg1, pg2, pg3, q_ref, new_ref, wint_ref, gate_ref, u_ref, x1_ref,
                        perm_ref, wkt_ref, wv_ref, tb_ref, wout_ref, gfin_ref, y_ref,
                        xl_s, kckt_s, kcv_s, skt_s, svt_s, bc_s, bs_s, bnew_s, bw_s, obr_s,
                        *, past, ds, wb):
    b = pl.program_id(0)
    pg = pl.program_id(1)
    npg = pl.num_programs(1)
    pages = (pg0, pg1, pg2, pg3)
    rows = Q_PER_KV * N_KV * ds
    nchunk = past // (CHUNK_PAGES * PAGE_SIZE)
    npages = past // PAGE_SIZE
    ncl = past // CMP_BLOCK
    nblk = past // SEL_BLOCK
    k_top = min(TOP_N - 1, nblk)
    nt = (((1,), (1,)), ((), ()))

    row_i = lax.broadcasted_iota(jnp.int32, (rows, 1), 0)
    tt = row_i % ds

    @pl.when((b == 0) & (pg == 0))
    def _():
        tbs = tb_ref[...] - tb_ref[:, N_BUCKETS - 1:N_BUCKETS]
        col = lambda k: tbs[:, k:k + 1]
        lane = lax.broadcasted_iota(jnp.int32, (1, ncl), 1)
        slot = lane % 8
        cblk = 8 * (lane // 8) + jnp.where(slot < 4, 2 * slot, 2 * (slot - 4) + 1)
        bc_s[...] = _bias_from_dist(past + tt - (CMP_BLOCK * cblk + CMP_BLOCK - 1), col)
        lane_p = lax.broadcasted_iota(jnp.int32, (1, PAGE_SIZE), 1)
        for k in range(NEAR_PAGES):
            bs_s[k] = _bias_from_dist(tt + MAX_DISTANCE - PAGE_SIZE * k - lane_p, col)
        bnew_s[...] = _bias_from_dist(tt - lax.broadcasted_iota(jnp.int32, (1, NEW_PAD), 1), col)
        bw_s[...] = _bias_from_dist(tt + wb - lax.broadcasted_iota(jnp.int32, (1, wb), 1), col)

    for pair in range(PAGES_PER_STEP // 2):
        pa, pb = pages[2 * pair], pages[2 * pair + 1]
        slab = pl.ds(pl.multiple_of(((pg % STEPS_PER_CHUNK) * (PAGES_PER_STEP // 2) + pair) * 8, 8), 8)
        for kind in range(2):
            xt2 = jnp.concatenate([pa[0, kind * 256:(kind + 1) * 256, :], pb[0, kind * 256:(kind + 1) * 256, :]],
                                  axis=1).astype(BF16)
            x_perm = lax.dot_general(perm_ref[...], xt2, nt, preferred_element_type=F32)
            xl_s[:, slab, kind * 256:(kind + 1) * 256] = x_perm.reshape(CMP_BLOCK, 8, 256)
    for k in range(PAGES_PER_STEP):
        skt_s[pg * PAGES_PER_STEP + k] = pages[k][0, 512:768, :].astype(BF16)
        svt_s[pg * PAGES_PER_STEP + k] = pages[k][0, 768:1024, :].astype(BF16)

    @pl.when(pg % STEPS_PER_CHUNK == STEPS_PER_CHUNK - 1)
    def _():
        nbc = CHUNK_PAGES * PAGE_SIZE // CMP_BLOCK

        def body(l, acc):
            ak, av = acc
            xk = xl_s[l, :, 0:256].astype(BF16)
            xv = xl_s[l, :, 256:512].astype(BF16)
            ak = ak + lax.dot_general(wkt_ref[l], xk, nt, preferred_element_type=F32)
            av = av + jnp.dot(xv, wv_ref[l], preferred_element_type=F32)
            return ak, av

        ak, av = lax.fori_loop(0, CMP_BLOCK, body, (jnp.zeros((256, nbc), F32), jnp.zeros((nbc, 256), F32)),
                               unroll=4)
        ch = pg // STEPS_PER_CHUNK
        kckt_s[ch] = ak.astype(BF16)
        kcv_s[ch] = av.astype(BF16)

    @pl.when(pg == npg - 1)
    def _():
        lane_g = lax.broadcasted_iota(jnp.int32, (1, 256), 1) // HEAD_DIM
        row_g = (row_i // ds) % N_KV
        wq = jnp.where(row_g == lane_g, q_ref[0], jnp.zeros((), BF16))
        t_row = past + tt

        s_c = jnp.concatenate([jnp.dot(wq, kckt_s[ch], preferred_element_type=F32) for ch in range(nchunk)],
                              axis=1) + bc_s[...]
        m = jnp.max(s_c, axis=1, keepdims=True)
        e = jnp.exp(s_c - m)
        p_c = e / jnp.sum(e, axis=1, keepdims=True)
        o_c = jnp.zeros((rows, 256), F32)
        for ch in range(nchunk):
            o_c = o_c + jnp.dot(p_c[:, ch * 128:(ch + 1) * 128].astype(BF16), kcv_s[ch], preferred_element_type=F32)
        obr_s[0] = o_c

        gt_rows = N_KV * ds
        ps = p_c[0:gt_rows]
        for r in range(1, Q_PER_KV):
            ps = ps + p_c[r * gt_rows:(r + 1) * gt_rows]
        imp = ps + pltpu.roll(ps, ncl - 4, axis=1)
        lane = lax.broadcasted_iota(jnp.int32, (gt_rows, ncl), 1)
        slot = lane % 8
        jblk = 4 * (lane // 8) + slot
        t_gt = past + lax.broadcasted_iota(jnp.int32, (gt_rows, ncl), 0) % ds
        forced = (jblk == 0) | (jblk == t_gt // SEL_BLOCK)
        valid = jblk * SEL_BLOCK <= t_gt
        scores = jnp.where(slot < 4, jnp.where(forced, jnp.inf, jnp.where(valid, imp, NEG_INF)), NEG_INF)
        jf = jnp.where(slot < 4, jblk, 2 * nblk).astype(F32)
        picks = []
        for _ in range(k_top):
            mx = jnp.max(scores, axis=1, keepdims=True)
            idx = jnp.min(jnp.where(scores == mx, jf, float(2 * nblk)), axis=1, keepdims=True)
            scores = jnp.where(jf == idx, NEG_INF, scores)
            picks.append(idx)

        gk = NEAR_PAGES
        blk_lane = (lax.broadcasted_iota(jnp.int32, (1, gk * PAGE_SIZE), 1) // SEL_BLOCK).astype(F32)

        def pages_step(pgi, carry, bias):
            m_old, l_old, acc = carry
            s = jnp.concatenate([jnp.dot(wq, skt_s[pgi * gk + k], preferred_element_type=F32) for k in range(gk)],
                                axis=1)
            if bias is not None:
                s = s + bias
            jk = blk_lane + jnp.asarray(pgi * (gk * PAGE_SIZE // SEL_BLOCK), F32)
            hit = jk == picks[0]
            for pk in picks[1:]:
                hit = hit | (jk == pk)
            pen = jnp.where(hit, 0.0, NEG_INF)
            s = (s.reshape(Q_PER_KV, gt_rows, gk * PAGE_SIZE) + pen[None]).reshape(rows, gk * PAGE_SIZE)
            m_new, alpha, l_new, pr = _softmax_lanes_online(s, m_old, l_old)
            prb = pr.astype(BF16)
            pv = alpha * acc
            for k in range(gk):
                pv = pv + lax.dot_general(prb[:, k * PAGE_SIZE:(k + 1) * PAGE_SIZE], svt_s[pgi * gk + k], nt,
                                          preferred_element_type=F32)
            return m_new, l_new, pv

        carry = (jnp.full((rows, 1), M_FLOOR, F32), jnp.zeros((rows, 1), F32), jnp.zeros((rows, 256), F32))
        carry = lax.fori_loop(0, npages // gk - 1, lambda j, c: pages_step(j, c, None), carry)
        bias_near = jnp.concatenate([bs_s[k] for k in range(NEAR_PAGES)], axis=1)
        m_old, l_old, acc = pages_step(npages // gk - 1, carry, bias_near)
        lane_n = lax.broadcasted_iota(jnp.int32, (1, NEW_PAD), 1)
        ok_new = (lane_n < ds) & (lane_n <= tt)
        s = lax.dot_general(wq, new_ref[0, 0], nt, preferred_element_type=F32) + bnew_s[...]
        m_new, alpha, l_new, pr = _softmax_lanes_online(jnp.where(ok_new, s, NEG_INF), m_old, l_old)
        acc = alpha * acc + jnp.dot(pr.astype(BF16), new_ref[0, 1], preferred_element_type=F32)
        obr_s[1] = acc / l_new

        kwt = wint_ref[0, 0:256, :].astype(BF16)
        vwt = wint_ref[0, 256:512, :].astype(BF16)
        dist = tt + wb - lax.broadcasted_iota(jnp.int32, (1, wb), 1)
        s_w = jnp.where((dist >= 0) & (dist < WINDOW), jnp.dot(wq, kwt, preferred_element_type=F32) + bw_s[...], NEG_INF)
        s_n = jnp.where(ok_new, lax.dot_general(wq, new_ref[0, 2], nt, preferred_element_type=F32) + bnew_s[...], NEG_INF)
        m = jnp.maximum(jnp.max(s_w, axis=1, keepdims=True), jnp.max(s_n, axis=1, keepdims=True))
        e_w = jnp.exp(s_w - m)
        e_n = jnp.exp(s_n - m)
        den = jnp.sum(e_w, axis=1, keepdims=True) + jnp.sum(e_n, axis=1, keepdims=True)
        o_w = (lax.dot_general(e_w.astype(BF16), vwt, nt, preferred_element_type=F32)
               + jnp.dot(e_n.astype(BF16), new_ref[0, 3], preferred_element_type=F32))
        obr_s[2] = o_w / den

        merged = []
        for r in range(Q_PER_KV):
            tot = None
            for br in range(3):
                o_r = jnp.zeros((ds, 256), F32)
                g_r = jnp.zeros((ds, 256), F32)
                for g in range(N_KV):
                    base = r * gt_rows + g * ds
                    o_r = jnp.where(lane_g == g, obr_s[br, base:base + ds, :], o_r)
                    c = br * N_HEADS + g * Q_PER_KV + r
                    g_r = jnp.where(lane_g == g, gate_ref[0, :, c:c + 1], g_r)
                tot = g_r * o_r if tot is None else tot + g_r * o_r
            merged.append(tot)
        op = jnp.concatenate(merged, axis=1)
        y = jnp.dot((op * _silu(u_ref[0])).astype(BF16), wout_ref[...], preferred_element_type=F32)
        y_ref[0] = _rms(x1_ref[0] + y, gfin_ref[...])


def _attn_sample(page_table, pool_t, q64, newkv, win_t, gates, u_r, x1, perm, wkt_bd, wv_bd, tb_rows, w_out_r, gfin,
                 *, past, ds, wb):
    nb = page_table.shape[0]
    npages = past // PAGE_SIZE
    assert past % (CHUNK_PAGES * PAGE_SIZE) == 0 and npages >= NEAR_PAGES and past // SEL_BLOCK >= TOP_N
    npg = npages // PAGES_PER_STEP
    nchunk = npages // CHUNK_PAGES
    rows = Q_PER_KV * N_KV * ds
    ncl = past // CMP_BLOCK
    once = dict(pipeline_mode=pl.Buffered(1))
    full = lambda shp, **kw: pl.BlockSpec(shp, lambda b, p, pt: (0,) * len(shp), **kw)
    per_b = lambda shp: pl.BlockSpec((1,) + shp, lambda b, p, pt: (b,) + (0,) * len(shp))
    page_specs = [pl.BlockSpec((1, 1024, PAGE_SIZE),
                               functools.partial(lambda b, p, pt, k: (pt[b, p * PAGES_PER_STEP + k], 0, 0), k=k))
                  for k in range(PAGES_PER_STEP)]
    grid_spec = pltpu.PrefetchScalarGridSpec(
        num_scalar_prefetch=1, grid=(nb, npg),
        in_specs=[*page_specs, per_b((rows, 256)), per_b((4, NEW_PAD, 256)), per_b((512, wb)),
                  per_b((ds, GATE_PAD)), per_b((ds, 1024)), per_b((ds, D_MODEL)),
                  full((256, 256), **once), full((CMP_BLOCK, 256, 256), **once), full((CMP_BLOCK, 256, 256), **once),
                  full((rows, N_BUCKETS)), full((1024, D_MODEL), **once), full((1, D_MODEL))],
        out_specs=pl.BlockSpec((1, ds, D_MODEL), lambda b, p, pt: (b, 0, 0)),
        scratch_shapes=[pltpu.VMEM((CMP_BLOCK, CHUNK_PAGES * PAGE_SIZE // CMP_BLOCK, 512), F32),
                        pltpu.VMEM((nchunk, 256, 128), BF16), pltpu.VMEM((nchunk, 128, 256), BF16),
                        pltpu.VMEM((npages, 256, PAGE_SIZE), BF16), pltpu.VMEM((npages, 256, PAGE_SIZE), BF16),
                        pltpu.VMEM((rows, ncl), F32), pltpu.VMEM((NEAR_PAGES, rows, PAGE_SIZE), F32),
                        pltpu.VMEM((rows, NEW_PAD), F32), pltpu.VMEM((rows, wb), F32),
                        pltpu.VMEM((3, rows, 256), F32)])
    return pl.pallas_call(
        functools.partial(_attn_sample_kernel, past=past, ds=ds, wb=wb),
        grid_spec=grid_spec,
        out_shape=jax.ShapeDtypeStruct((nb, ds, D_MODEL), F32),
        compiler_params=_cparams("arbitrary", "arbitrary"),
        name="nsa_attn_sample",
    )(page_table, *([pool_t] * PAGES_PER_STEP), q64, newkv, win_t, gates, u_r, x1, perm, wkt_bd, wv_bd, tb_rows,
      w_out_r, gfin)


def _pair_permutation():
    m = jnp.arange(256)
    l, slot = m // 8, m % 8
    cb = jnp.where(slot < 4, 2 * slot, 2 * (slot - 4) + 1)
    tok = CMP_BLOCK * cb + l
    return (tok[:, None] == jnp.arange(256)[None, :]).astype(BF16)


def _block_diag4(w):
    eye = jnp.eye(N_KV, dtype=w.dtype)
    out = eye[:, None, :, None] * w[..., None, :, None, :]
    return out.reshape(*w.shape[:-2], N_KV * HEAD_DIM, N_KV * HEAD_DIM)


def kernel(x_prompt, x_sample, state_lru_h, state_conv, cache_nsa_kv, cache_win_kv, page_table, norm_g,
           final_norm_g, w_in_lru, conv_w, conv_b, w_gate_a, b_gate_a, w_gate_x, b_gate_x, lru_lambda,
           w_out_lru, w_in_nsa, w_cmp, w_out_nsa, rel_bias):
    bp, s_len, _ = x_prompt.shape
    nb, ds, _ = x_sample.shape
    assert bp == 1
    row = lambda v: v.reshape(1, -1)

    lru_w = (row(norm_g[0]), w_in_lru[0].astype(BF16), conv_w[0], row(conv_b[0]), w_gate_a[0].astype(BF16),
             row(b_gate_a[0]), w_gate_x[0].astype(BF16), row(b_gate_x[0]), row(lru_lambda[0]),
             w_out_lru[0].astype(BF16))
    x1p, hp, tailp = _lru_prompt(x_prompt[0], *lru_w)
    xs_tm = x_sample.transpose(1, 0, 2).reshape(ds * nb, D_MODEL)
    buf_tm = state_conv[0].transpose(1, 0, 2).reshape((CONV_W - 1) * nb, D_MODEL)
    x1s_tm, hs_last, tails = _lru_sample(xs_tm, buf_tm, state_lru_h[0], *lru_w, nb=nb, ds=ds)

    prompt_lru_h = hp.reshape(1, 1, D_MODEL)
    prompt_conv = tailp[8 - (CONV_W - 1):].reshape(1, 1, CONV_W - 1, D_MODEL)
    sample_lru_h = hs_last.reshape(1, nb, D_MODEL)
    sample_conv = tails.reshape(CONV_W - 1, nb, D_MODEL).transpose(1, 0, 2)[None]

    w = w_in_nsa[0]
    wq, wkv = w[:, :1024], w[:, 1024:1024 + KV_WIDTH]
    wg, wu = w[:, 1024 + KV_WIDTH:1024 + KV_WIDTH + 3 * N_HEADS], w[:, 1024 + KV_WIDTH + 3 * N_HEADS:]
    wg_pad = jnp.pad(wg, ((0, 0), (0, GATE_PAD - 3 * N_HEADS)))
    w_nat_p = jnp.concatenate([wkv[:, 0:768], wkv[:, 1024:1280], wu], axis=1).astype(BF16)
    w_t_p = jnp.concatenate([wq.T, wkv.T, wg_pad.T], axis=0).astype(BF16)
    wk_bd = _block_diag4(w_cmp[0, 0]).astype(BF16)
    wvt_bd = jnp.swapaxes(_block_diag4(w_cmp[0, 1]), -1, -2).astype(BF16)
    w_out = w_out_nsa[0].astype(BF16)
    g1 = row(norm_g[1])
    gfin = row(final_norm_g)

    cmp_kv, selk, wink, u_p, qt, kvcst, kvwt, selvt, winvt, gt = _proj_prompt(x1p, g1, w_nat_p, w_t_p)
    kck, kcvt = _compress_prompt(cmp_kv, wk_bd, wvt_bd)
    y_prompt = _attn_prompt(rel_bias, qt, kck, kcvt, selk, selvt, wink, winvt, gt, u_p, x1p, w_out, gfin)
    y_prompt = y_prompt[None]
    prompt_kv_rows = kvcst.reshape(4, N_KV, HEAD_DIM, s_len).transpose(3, 0, 1, 2)[None, None]
    wbp = min(WINDOW, s_len)
    prompt_win_kv = kvwt[:, s_len - wbp:].reshape(2, N_KV, HEAD_DIM, wbp).transpose(3, 0, 1, 2)[None, None]

    n_phys, n_layers = cache_nsa_kv.shape[:2]
    past = page_table.shape[1] * PAGE_SIZE
    wb = cache_win_kv.shape[2]
    to_rgd = lambda m: m.reshape(m.shape[0], N_KV, Q_PER_KV, HEAD_DIM).transpose(0, 2, 1, 3).reshape(m.shape[0], 1024)
    w_nat_s = jnp.concatenate([wkv, to_rgd(wq), to_rgd(wu), wg_pad], axis=1).astype(BF16)
    w_out_r = to_rgd(w_out_nsa[0].T).T.astype(BF16)
    x1s = x1s_tm.reshape(ds, nb, D_MODEL).transpose(1, 0, 2).reshape(nb * ds, D_MODEL)
    kvcs_s, kvw_s, q_s, u_s, gate_s = _proj_sample(x1s, g1, w_nat_s)
    q64 = q_s.reshape(nb, ds, Q_PER_KV, 1, 256).transpose(0, 2, 3, 1, 4)
    q64 = jnp.broadcast_to(q64, (nb, Q_PER_KV, N_KV, ds, 256)).reshape(nb, Q_PER_KV * N_KV * ds, 256)
    new4 = jnp.stack([kvcs_s[:, 512:768], kvcs_s[:, 768:1024], kvw_s[:, 0:256], kvw_s[:, 256:512]], axis=0)
    new4 = new4.reshape(4, nb, ds, 256).transpose(1, 0, 2, 3).astype(BF16)
    newkv = jnp.pad(new4, ((0, 0), (0, 0), (0, NEW_PAD - ds), (0, 0)))
    pool_t = cache_nsa_kv.transpose(0, 1, 3, 4, 5, 2).reshape(n_phys * n_layers, 1024, PAGE_SIZE)
    win_t = cache_win_kv[0].transpose(0, 2, 3, 4, 1).reshape(nb, 512, wb)
    head_of_row = (jnp.arange(Q_PER_KV * N_KV * ds) // ds % N_KV) * Q_PER_KV + jnp.arange(Q_PER_KV * N_KV * ds) // (N_KV * ds)
    tb_rows = rel_bias.T[head_of_row]
    y_sample = _attn_sample(page_table * n_layers, pool_t, q64, newkv, win_t, gate_s.reshape(nb, ds, GATE_PAD),
                            u_s.reshape(nb, ds, 1024), x1s.reshape(nb, ds, D_MODEL), _pair_permutation(),
                            jnp.swapaxes(wk_bd, -1, -2), jnp.swapaxes(wvt_bd, -1, -2), tb_rows, w_out_r, gfin,
                            past=past, ds=ds, wb=wb)
    sample_kv_rows = kvcs_s.reshape(nb, 1, ds, 4, N_KV, HEAD_DIM)
    new_t = kvw_s.reshape(nb, ds, 512).transpose(0, 2, 1)
    win_all = jnp.concatenate([win_t, new_t], axis=2)[:, :, ds:]
    sample_win_kv = win_all.reshape(nb, 2, N_KV, HEAD_DIM, wb).transpose(0, 4, 1, 2, 3)[None]

    return (y_prompt, y_sample, prompt_lru_h, prompt_conv, prompt_kv_rows, prompt_win_kv,
            sample_lru_h, sample_conv, sample_kv_rows, sample_win_kv)
```

```python
import functools
import math

import jax
import jax.numpy as jnp
from jax import lax
from jax.experimental import pallas as pl
from jax.experimental.pallas import tpu as pltpu

F32 = jnp.float32
BF16 = jnp.bfloat16

D_MODEL = 1024
HEAD_DIM = 64
N_HEADS = 16
N_KV = 4
Q_PER_KV = 4
CMP_BLOCK = 32
SEL_BLOCK = 64
TOP_N = 16
WINDOW = 512
PAGE_SIZE = 128
N_BUCKETS = 32
MAX_DISTANCE = 1024
N_LRU_BLOCKS = 8
LRU_BLOCK = 128
CONV_W = 4
LRU_C = 8.0
RMS_EPS = 1e-6

V7X_VMEM_BYTES = 64 * 1024 * 1024
VMEM_LIMIT = V7X_VMEM_BYTES - 8 * 1024 * 1024


def _bucket_lower_bounds():
    max_exact = N_BUCKETS // 2
    lows = list(range(1, max_exact + 1))
    for k in range(1, N_BUCKETS - max_exact):
        n = max_exact
        while n ** 8 < 2 ** (32 + 3 * k):
            n += 1
        lows.append(n)
    return lows


_BUCKET_LOWS = _bucket_lower_bounds()


def _bias_from_dist(dist, table_col):
    val = jnp.full(dist.shape, table_col(0), F32)
    for b, lo in enumerate(_BUCKET_LOWS, start=1):
        val = jnp.where(dist >= lo, table_col(b), val)
    return val


def _rms(x, g):
    ms = jnp.mean(x * x, axis=-1, keepdims=True)
    return x * lax.rsqrt(ms + RMS_EPS) * g


def _sigmoid(x):
    return 1.0 / (1.0 + jnp.exp(-x))


def _silu(x):
    return x * _sigmoid(x)


def _softplus(x):
    return jnp.maximum(x, 0.0) + jnp.log1p(jnp.exp(-jnp.abs(x)))


def _expm1(x):
    u = jnp.exp(x)
    um1 = u - 1.0
    y = um1 * x / jnp.log(u)
    y = jnp.where(u == 1.0, x, y)
    return jnp.where(um1 == -1.0, -1.0, y)


def _cparams(*sem):
    return pltpu.CompilerParams(dimension_semantics=sem, vmem_limit_bytes=VMEM_LIMIT)


def _lru_gate_terms(conv, wa_ref, ba, wx_ref, bx, lam):
    cb = conv.astype(BF16)
    r_parts, i_parts = [], []
    for n in range(N_LRU_BLOCKS):
        c = cb[:, n * LRU_BLOCK:(n + 1) * LRU_BLOCK]
        r_parts.append(jnp.dot(c, wa_ref[n], preferred_element_type=F32))
        i_parts.append(jnp.dot(c, wx_ref[n], preferred_element_type=F32))
    r = _sigmoid(jnp.concatenate(r_parts, axis=1) + ba)
    i = _sigmoid(jnp.concatenate(i_parts, axis=1) + bx)
    log_a = -LRU_C * r * _softplus(-lam)
    a = jnp.exp(log_a)
    u = jnp.sqrt(-_expm1(2.0 * log_a)) * i * conv
    return a, u


def _lru_prompt_kernel(x_ref, g_ref, win_ref, cw_ref, cb_ref, wa_ref, ba_ref, wx_ref, bx_ref,
                       lam_ref, wout_ref, x1_ref, hlast_ref, tail_ref,
                       xb_ext, a_s, u_s, h_c, *, tr):
    i = pl.program_id(0)

    @pl.when(i == 0)
    def _():
        xb_ext[0:8, :] = jnp.zeros((8, D_MODEL), F32)
        h_c[...] = jnp.zeros_like(h_c)

    x = x_ref[...]
    xn = _rms(x, g_ref[...])
    z = jnp.dot(xn.astype(BF16), win_ref[...], preferred_element_type=F32)
    xb = z[:, :D_MODEL]
    gate = z[:, D_MODEL:]
    xb_ext[8:8 + tr, :] = xb
    conv = cb_ref[...]
    for j in range(CONV_W):
        conv = conv + cw_ref[j:j + 1, :] * xb_ext[8 - (CONV_W - 1) + j:8 - (CONV_W - 1) + j + tr, :]
    a, u = _lru_gate_terms(conv, wa_ref, ba_ref[...], wx_ref, bx_ref[...], lam_ref[...])
    a_s[...] = a
    u_s[...] = u

    row = lax.broadcasted_iota(jnp.int32, (8, D_MODEL), 0)

    def body(k, h):
        base = pl.multiple_of(k * 8, 8)
        aa = a_s[pl.ds(base, 8), :]
        uu = u_s[pl.ds(base, 8), :]
        for s in (1, 2, 4):
            keep = row >= s
            us = jnp.where(keep, pltpu.roll(uu, s, axis=0), 0.0)
            as_ = jnp.where(keep, pltpu.roll(aa, s, axis=0), 1.0)
            uu = uu + aa * us
            aa = aa * as_
        hb = aa * h + uu
        u_s[pl.ds(base, 8), :] = hb
        return hb[7:8, :]

    h_last = lax.fori_loop(0, tr // 8, body, h_c[...])
    h_c[...] = h_last
    hlast_ref[...] = h_last
    tail_ref[...] = xb[tr - 8:, :]
    xb_ext[0:8, :] = xb[tr - 8:, :]
    y = u_s[...] * _silu(gate)
    out = jnp.dot(y.astype(BF16), wout_ref[...], preferred_element_type=F32)
    x1_ref[...] = x + out


def _lru_prompt(x, g, w_in, cw, cb, wa, ba, wx, bx, lam, w_out, tr=512):
    s = x.shape[0]
    tr = min(tr, s)
    assert s % tr == 0 and tr % 8 == 0
    full = lambda shp: pl.BlockSpec(shp, lambda i: (0,) * len(shp))
    return pl.pallas_call(
        functools.partial(_lru_prompt_kernel, tr=tr),
        grid=(s // tr,),
        in_specs=[pl.BlockSpec((tr, D_MODEL), lambda i: (i, 0)), full((1, D_MODEL)),
                  full((D_MODEL, 2 * D_MODEL)), full((CONV_W, D_MODEL)), full((1, D_MODEL)),
                  full((N_LRU_BLOCKS, LRU_BLOCK, LRU_BLOCK)), full((1, D_MODEL)),
                  full((N_LRU_BLOCKS, LRU_BLOCK, LRU_BLOCK)), full((1, D_MODEL)),
                  full((1, D_MODEL)), full((D_MODEL, D_MODEL))],
        out_specs=[pl.BlockSpec((tr, D_MODEL), lambda i: (i, 0)), full((1, D_MODEL)), full((8, D_MODEL))],
        out_shape=[jax.ShapeDtypeStruct((s, D_MODEL), F32), jax.ShapeDtypeStruct((1, D_MODEL), F32),
                   jax.ShapeDtypeStruct((8, D_MODEL), F32)],
        scratch_shapes=[pltpu.VMEM((tr + 8, D_MODEL), F32), pltpu.VMEM((tr, D_MODEL), F32),
                        pltpu.VMEM((tr, D_MODEL), F32), pltpu.VMEM((1, D_MODEL), F32)],
        compiler_params=_cparams("arbitrary"),
        name="lru_prompt",
    )(x, g, w_in, cw, cb, wa, ba, wx, bx, lam, w_out)


def _lru_sample_kernel(x_ref, buf_ref, h0_ref, g_ref, win_ref, cw_ref, cb_ref, wa_ref, ba_ref, wx_ref,
                       bx_ref, lam_ref, wout_ref, x1_ref, hlast_ref, tail_ref, xp_s, *, nb, ds):
    x = x_ref[...]
    xn = _rms(x, g_ref[...])
    z = jnp.dot(xn.astype(BF16), win_ref[...], preferred_element_type=F32)
    xb = z[:, :D_MODEL]
    gate = z[:, D_MODEL:]
    nbuf = (CONV_W - 1) * nb
    xp_s[0:nbuf, :] = buf_ref[...]
    xp_s[nbuf:, :] = xb
    conv = cb_ref[...]
    for j in range(CONV_W):
        conv = conv + cw_ref[j:j + 1, :] * xp_s[j * nb:j * nb + ds * nb, :]
    a, u = _lru_gate_terms(conv, wa_ref, ba_ref[...], wx_ref, bx_ref[...], lam_ref[...])
    h = h0_ref[...]
    hs = []
    for t in range(ds):
        h = a[t * nb:(t + 1) * nb, :] * h + u[t * nb:(t + 1) * nb, :]
        hs.append(h)
    hlast_ref[...] = h
    tail_ref[...] = xp_s[ds * nb:, :]
    y = jnp.concatenate(hs, axis=0) * _silu(gate)
    out = jnp.dot(y.astype(BF16), wout_ref[...], preferred_element_type=F32)
    x1_ref[...] = x + out


def _lru_sample(x_tm, buf_tm, h0, g, w_in, cw, cb, wa, ba, wx, bx, lam, w_out, nb, ds):
    rows = nb * ds
    nbuf = (CONV_W - 1) * nb
    return pl.pallas_call(
        functools.partial(_lru_sample_kernel, nb=nb, ds=ds),
        out_shape=[jax.ShapeDtypeStruct((rows, D_MODEL), F32), jax.ShapeDtypeStruct((nb, D_MODEL), F32),
                   jax.ShapeDtypeStruct((nbuf, D_MODEL), F32)],
        scratch_shapes=[pltpu.VMEM((nbuf + rows, D_MODEL), F32)],
        compiler_params=pltpu.CompilerParams(vmem_limit_bytes=VMEM_LIMIT),
        name="lru_sample",
    )(x_tm, buf_tm, h0, g, w_in, cw, cb, wa, ba, wx, bx, lam, w_out)


Q_BLOCK = 128
LANES_G = Q_PER_KV * Q_BLOCK
KEY_TILE = 512
NEAR_PAD = KEY_TILE
NEAR_ROWS = NEAR_PAD + MAX_DISTANCE + KEY_TILE
NEG_INF = float("-inf")
LOG2E = math.log2(math.e)
M_FLOOR = -1e30

KV_WIDTH = 3 * 2 * N_KV * HEAD_DIM
GATE_PAD = 128
V_ROWS = HEAD_DIM + 16


def _proj_prompt_kernel(x_ref, g_ref, wn_ref, wt_ref, cmp_ref, selk_ref, wink_ref, u_ref,
                        qt_ref, kvcst_ref, kvwt_ref, selvt_ref, winvt_ref, gt_ref, *, tr):
    xn = _rms(x_ref[...], g_ref[...]).astype(BF16)
    z = jnp.dot(xn, wn_ref[...], preferred_element_type=F32)
    cmp_ref[...] = z[:, 0:512].astype(BF16)
    for p in range(2):
        selk_ref[p] = z[:, 512 + 128 * p:640 + 128 * p].astype(BF16)
        wink_ref[p] = z[:, 768 + 128 * p:896 + 128 * p].astype(BF16)
    u_ref[...] = z[:, 1024:2048]
    zt = lax.dot_general(wt_ref[...], xn, (((1,), (1,)), ((), ())), preferred_element_type=F32)
    qt_ref[...] = (zt[0:1024] * (HEAD_DIM ** -0.5 * LOG2E)).astype(BF16).reshape(N_KV, Q_PER_KV * HEAD_DIM, tr)
    kvt = zt[1024:1024 + KV_WIDTH]
    kvcst_ref[...] = kvt[0:1024]
    kvwt_ref[...] = kvt[1024:1536]
    selvt_ref[:, 0:HEAD_DIM, :] = kvt[768:1024].astype(BF16).reshape(N_KV, HEAD_DIM, tr)
    pad_row = lax.broadcasted_iota(jnp.int32, (N_KV, V_ROWS - HEAD_DIM, tr), 1)
    selvt_ref[:, HEAD_DIM:V_ROWS, :] = jnp.where(pad_row == 0, 1.0, 0.0).astype(BF16)
    winvt_ref[...] = kvt[1280:1536].astype(BF16).reshape(N_KV, HEAD_DIM, tr)
    gt_ref[...] = _sigmoid(zt[1024 + KV_WIDTH:1024 + KV_WIDTH + GATE_PAD])


def _proj_prompt(x1, g, w_nat, w_t, tr=512):
    s = x1.shape[0]
    tr = min(tr, s)
    nn, nt = w_nat.shape[1], w_t.shape[0]
    full = lambda shp: pl.BlockSpec(shp, lambda i: (0,) * len(shp))
    sds = jax.ShapeDtypeStruct
    return pl.pallas_call(
        functools.partial(_proj_prompt_kernel, tr=tr),
        grid=(s // tr,),
        in_specs=[pl.BlockSpec((tr, D_MODEL), lambda i: (i, 0)), full((1, D_MODEL)), full((D_MODEL, nn)),
                  full((nt, D_MODEL))],
        out_specs=[pl.BlockSpec((tr, 512), lambda i: (i, 0)),
                   pl.BlockSpec((2, tr, 128), lambda i: (0, i, 0)), pl.BlockSpec((2, tr, 128), lambda i: (0, i, 0)),
                   pl.BlockSpec((tr, 1024), lambda i: (i, 0)),
                   pl.BlockSpec((N_KV, 256, tr), lambda i: (0, 0, i)),
                   pl.BlockSpec((1024, tr), lambda i: (0, i)), pl.BlockSpec((512, tr), lambda i: (0, i)),
                   pl.BlockSpec((N_KV, V_ROWS, tr), lambda i: (0, 0, i)),
                   pl.BlockSpec((N_KV, HEAD_DIM, tr), lambda i: (0, 0, i)),
                   pl.BlockSpec((GATE_PAD, tr), lambda i: (0, i))],
        out_shape=[sds((s, 512), BF16), sds((2, s, 128), BF16), sds((2, s, 128), BF16),
                   sds((s, 1024), F32), sds((N_KV, 256, s), BF16), sds((1024, s), F32), sds((512, s), F32),
                   sds((N_KV, V_ROWS, s), BF16), sds((N_KV, HEAD_DIM, s), BF16), sds((GATE_PAD, s), F32)],
        compiler_params=_cparams("arbitrary"),
        name="nsa_proj_prompt",
    )(x1, g, w_nat, w_t)


def _compress_prompt_kernel(xe_ref, xo_ref, wk_ref, wvt_ref, kck_ref, kcvt_ref, acck, accv, *, tl, nh):
    step = pl.program_id(0)

    @pl.when(step == 0)
    def _():
        acck[...] = jnp.zeros_like(acck)
        accv[...] = jnp.zeros_like(accv)

    for half, x_ref in enumerate((xe_ref, xo_ref)):
        for l in range(tl):
            xk = x_ref[:, l * 512:l * 512 + 256]
            xv = x_ref[:, l * 512 + 256:l * 512 + 512]
            acck[half * nh:(half + 1) * nh, :] += jnp.dot(xk, wk_ref[l], preferred_element_type=F32)
            accv[:, half * nh:(half + 1) * nh] += lax.dot_general(
                wvt_ref[l], xv, (((1,), (1,)), ((), ())), preferred_element_type=F32)

    @pl.when(step == pl.num_programs(0) - 1)
    def _():
        for p in range(2):
            kck_ref[p] = acck[:, 128 * p:128 * (p + 1)].astype(BF16)
        kcvt_ref[...] = accv[...].astype(BF16).reshape(N_KV, HEAD_DIM, 2 * nh)


def _compress_prompt(cmp_kv, wk_bd, wvt_bd, tl=4):
    s = cmp_kv.shape[0]
    nh = s // (2 * CMP_BLOCK)
    x2 = cmp_kv.reshape(nh, 2 * CMP_BLOCK * 512)
    nsteps = CMP_BLOCK // tl
    return pl.pallas_call(
        functools.partial(_compress_prompt_kernel, tl=tl, nh=nh),
        grid=(nsteps,),
        in_specs=[pl.BlockSpec((nh, tl * 512), lambda i: (0, i)),
                  pl.BlockSpec((nh, tl * 512), lambda i: (0, nsteps + i)),
                  pl.BlockSpec((tl, 256, 256), lambda i: (i, 0, 0)),
                  pl.BlockSpec((tl, 256, 256), lambda i: (i, 0, 0))],
        out_specs=[pl.BlockSpec((2, 2 * nh, 128), lambda i: (0, 0, 0)),
                   pl.BlockSpec((N_KV, HEAD_DIM, 2 * nh), lambda i: (0, 0, 0))],
        out_shape=[jax.ShapeDtypeStruct((2, 2 * nh, 128), BF16),
                   jax.ShapeDtypeStruct((N_KV, HEAD_DIM, 2 * nh), BF16)],
        scratch_shapes=[pltpu.VMEM((2 * nh, 256), F32), pltpu.VMEM((256, 2 * nh), F32)],
        compiler_params=_cparams("arbitrary"),
        name="nsa_compress_prompt",
    )(x2, x2, wk_bd, wvt_bd)


def _softmax_rows(s):
    m = jnp.max(s, axis=0, keepdims=True)
    m = jnp.where(m == NEG_INF, 0.0, m)
    e = jnp.exp2(s - m)
    den = jnp.sum(e, axis=0, keepdims=True)
    return e / jnp.where(den > 0, den, 1.0)


def _attn_prompt_kernel(bias_ref, qt_ref, kck_ref, kcvt_ref, selk_ref, selvt_ref,
                        wk0, wk1, wk2, wk3, wk4, wv0, wv1, wv2, wv3, wv4,
                        gt_ref, u_ref, x1_ref, wout_ref, gfin_ref, y_ref,
                        nb_s, wq_s, pen_s, sc_s, s0_s, obr_s, opt_s, *, s_len):
    i = pl.program_id(0)
    q0 = i * Q_BLOCK
    nc = s_len // CMP_BLOCK
    nch = nc // 2
    nblk = s_len // SEL_BLOCK
    k_top = min(TOP_N, nblk)
    wrows = min(32, nch)
    wk_refs = (wk0, wk1, wk2, wk3, wk4)
    wv_refs = (wv0, wv1, wv2, wv3, wv4)
    n_win = len(wk_refs)

    def table(h):
        return lambda b: (bias_ref[b, h] - bias_ref[N_BUCKETS - 1, h]) * LOG2E

    @pl.when(i == 0)
    def _():
        wq_s[...] = jnp.zeros_like(wq_s)
        rho = lax.broadcasted_iota(jnp.int32, (NEAR_ROWS, Q_BLOCK), 0)
        a = lax.broadcasted_iota(jnp.int32, (NEAR_ROWS, Q_BLOCK), 1)
        dist = a + (MAX_DISTANCE + NEAR_PAD) - rho
        for h in range(N_HEADS):
            nb_s[h] = _bias_from_dist(dist, table(h))

    for g in range(N_KV):
        off = (g % 2) * HEAD_DIM
        for r in range(Q_PER_KV):
            wq_s[g, off:off + HEAD_DIM, r * Q_BLOCK:(r + 1) * Q_BLOCK] = qt_ref[g, r * HEAD_DIM:(r + 1) * HEAD_DIM, :]

    lane_q = lax.broadcasted_iota(jnp.int32, (1, LANES_G), 1) % Q_BLOCK
    t_lane = q0 + lane_q
    blk = lax.broadcasted_iota(jnp.int32, (nblk, Q_BLOCK), 0)
    tq = lax.broadcasted_iota(jnp.int32, (nblk, Q_BLOCK), 1) + q0
    forced = (blk == 0) | (blk == tq // SEL_BLOCK)
    valid = blk * SEL_BLOCK <= tq
    blk_f = blk.astype(F32)

    def group_body(g, carry):
        wq = wq_s[g]
        pair = g // 2

        sc_s[...] = jnp.dot(kck_ref[pair], wq, preferred_element_type=F32)
        w0 = jnp.clip(((2 * i - 16) // 8) * 8, 0, nch - wrows)
        w0 = pl.multiple_of(w0, 8)
        jj = lax.broadcasted_iota(jnp.int32, (wrows, Q_BLOCK), 0) + w0
        aq = lax.broadcasted_iota(jnp.int32, (wrows, Q_BLOCK), 1) + q0
        for half in range(2):
            dist = aq - SEL_BLOCK * jj - (CMP_BLOCK * half + CMP_BLOCK - 1)
            rows = pl.ds(pl.multiple_of(half * nch + w0, 8), wrows)
            for r in range(Q_PER_KV):
                sc_s[rows, r * Q_BLOCK:(r + 1) * Q_BLOCK] += _bias_from_dist(dist, table(g * Q_PER_KV + r))
        rho = lax.broadcasted_iota(jnp.int32, (nc, 1), 0)
        ends = jnp.where(rho >= nch, SEL_BLOCK * (rho - nch) + 2 * CMP_BLOCK - 1, SEL_BLOCK * rho + CMP_BLOCK - 1)
        p_c = _softmax_rows(jnp.where(ends <= t_lane, sc_s[...], NEG_INF))
        obr_s[0, g] = jnp.dot(kcvt_ref[g], p_c.astype(BF16), preferred_element_type=F32)

        psum = p_c[:, 0:Q_BLOCK]
        for r in range(1, Q_PER_KV):
            psum = psum + p_c[:, r * Q_BLOCK:(r + 1) * Q_BLOCK]
        imp = psum[0:nch] + psum[nch:nc]
        pen_s[g, :, 0:Q_BLOCK] = jnp.where(forced, jnp.inf, jnp.where(valid, imp, NEG_INF))

        kw = jnp.concatenate([wk_refs[k][pair] for k in range(n_win)], axis=0)
        s = jnp.dot(kw, wq, preferred_element_type=F32)
        nw = n_win * Q_BLOCK
        r_w = NEAR_PAD + MAX_DISTANCE - WINDOW
        bias = jnp.concatenate([nb_s[g * Q_PER_KV + r, r_w:r_w + nw, :] for r in range(Q_PER_KV)], axis=1)
        loc = lax.broadcasted_iota(jnp.int32, (nw, 1), 0)
        dist = lane_q + WINDOW - loc
        ok = (dist >= 0) & (dist < WINDOW) & (q0 - WINDOW + loc >= 0)
        p_w = _softmax_rows(jnp.where(ok, s + bias, NEG_INF))
        vw = jnp.concatenate([wv_refs[k][g] for k in range(n_win)], axis=1)
        obr_s[2, g] = jnp.dot(vw, p_w.astype(BF16), preferred_element_type=F32)
        return carry

    lax.fori_loop(0, N_KV, group_body, 0)

    def pick(_, c):
        for g in range(N_KV):
            sc = pen_s[g, :, 0:Q_BLOCK]
            m = jnp.max(sc, axis=0, keepdims=True)
            idx = jnp.min(jnp.where(sc == m, blk_f, float(nblk)), axis=0, keepdims=True)
            pen_s[g, :, 0:Q_BLOCK] = jnp.where(blk_f == idx, NEG_INF, sc)
        return c

    lax.fori_loop(0, k_top, pick, 0)
    for g in range(N_KV):
        picked = (pen_s[g, :, 0:Q_BLOCK] == NEG_INF) & (forced | valid)
        pen = jnp.where(picked, 0.0, NEG_INF)
        pen_s[g] = jnp.concatenate([pen] * Q_PER_KV, axis=1)

    n_tiles = (q0 + Q_BLOCK - 1) // KEY_TILE + 1
    n_far = jnp.maximum(q0 - MAX_DISTANCE, 0) // KEY_TILE
    nbt = KEY_TILE // SEL_BLOCK

    def qk(g, tt):
        k0 = pl.multiple_of(tt * KEY_TILE, KEY_TILE)
        return jnp.dot(selk_ref[g // 2, pl.ds(k0, KEY_TILE), :], wq_s[g], preferred_element_type=F32)

    def tile(tt, state, near):
        k0 = pl.multiple_of(tt * KEY_TILE, KEY_TILE)
        if near:
            r0 = pl.multiple_of(jnp.maximum(k0 - q0 + MAX_DISTANCE + NEAR_PAD, 0), Q_BLOCK)
            causal = k0 + lax.broadcasted_iota(jnp.int32, (KEY_TILE, 1), 0) <= t_lane
        out = []
        s_next = s0_s[...]
        for g in range(N_KV):
            s = s_next
            s_next = qk(g + 1, tt) if g + 1 < N_KV else qk(0, jnp.minimum(tt + 1, n_tiles - 1))
            m_old, acc = state[g]
            if near:
                bias = jnp.concatenate(
                    [nb_s[g * Q_PER_KV + r, pl.ds(r0, KEY_TILE), :] for r in range(Q_PER_KV)], axis=1)
                s = jnp.where(causal, s + bias, NEG_INF)
            pen_t = pen_s[g, pl.ds(pl.multiple_of(tt * nbt, nbt), nbt), :]
            s = (s.reshape(nbt, SEL_BLOCK, LANES_G) + pen_t[:, None, :]).reshape(KEY_TILE, LANES_G)
            m_new = jnp.maximum(m_old, jnp.max(s, axis=0, keepdims=True))
            alpha = jnp.exp2(m_old - m_new)
            p = jnp.exp2(s - m_new).astype(BF16)
            pv = jnp.dot(selvt_ref[g, :, pl.ds(k0, KEY_TILE)], p, preferred_element_type=F32)
            out.append((m_new, alpha * acc + pv))
        s0_s[...] = s_next
        return tuple(out)

    s0_s[...] = qk(0, 0)
    init = tuple((jnp.full((1, LANES_G), M_FLOOR, F32), jnp.zeros((V_ROWS, LANES_G), F32)) for _ in range(N_KV))
    far_done = lax.fori_loop(0, n_far, lambda tt, c: tile(tt, c, False), init)
    sel_done = lax.fori_loop(n_far, n_tiles, lambda tt, c: tile(tt, c, True), far_done)
    for g in range(N_KV):
        acc = sel_done[g][1]
        obr_s[1, g] = acc[0:HEAD_DIM] / acc[HEAD_DIM:HEAD_DIM + 1]

    for g in range(N_KV):
        for r in range(Q_PER_KV):
            h = g * Q_PER_KV + r
            sl = slice(r * Q_BLOCK, (r + 1) * Q_BLOCK)
            merged = (gt_ref[h:h + 1, :] * obr_s[0, g, :, sl] + gt_ref[N_HEADS + h:N_HEADS + h + 1, :] * obr_s[1, g, :, sl]
                      + gt_ref[2 * N_HEADS + h:2 * N_HEADS + h + 1, :] * obr_s[2, g, :, sl])
            opt_s[h * HEAD_DIM:(h + 1) * HEAD_DIM, :] = merged
    op = opt_s[...].T
    y = jnp.dot((op * _silu(u_ref[...])).astype(BF16), wout_ref[...], preferred_element_type=F32)
    y_ref[...] = _rms(x1_ref[...] + y, gfin_ref[...])


def _attn_prompt(rel_bias, qt, kck, kcvt, selk, selvt, wink, winvt, gt, u, x1, w_out, gfin):
    s = x1.shape[0]
    nq = s // Q_BLOCK
    nc = s // CMP_BLOCK
    nblk = s // SEL_BLOCK
    assert s % (2 * CMP_BLOCK * 128) == 0 and nblk >= TOP_N
    n_win = WINDOW // Q_BLOCK + 1
    full = lambda shp, **kw: pl.BlockSpec(shp, lambda i: (0,) * len(shp), **kw)
    once = dict(pipeline_mode=pl.Buffered(1))
    win_k_specs = [pl.BlockSpec((2, Q_BLOCK, 128), functools.partial(lambda i, k: (0, jnp.maximum(i - (n_win - 1) + k, 0), 0), k=k))
                   for k in range(n_win)]
    win_v_specs = [pl.BlockSpec((N_KV, HEAD_DIM, Q_BLOCK), functools.partial(lambda i, k: (0, 0, jnp.maximum(i - (n_win - 1) + k, 0)), k=k))
                   for k in range(n_win)]
    return pl.pallas_call(
        functools.partial(_attn_prompt_kernel, s_len=s),
        grid=(nq,),
        in_specs=[pl.BlockSpec(memory_space=pltpu.SMEM),
                  pl.BlockSpec((N_KV, 256, Q_BLOCK), lambda i: (0, 0, i)),
                  full((2, nc, 128), **once), full((N_KV, HEAD_DIM, nc), **once),
                  full((2, s, 128), **once), full((N_KV, V_ROWS, s), **once),
                  *win_k_specs, *win_v_specs,
                  pl.BlockSpec((GATE_PAD, Q_BLOCK), lambda i: (0, i)),
                  pl.BlockSpec((Q_BLOCK, 1024), lambda i: (i, 0)),
                  pl.BlockSpec((Q_BLOCK, D_MODEL), lambda i: (i, 0)),
                  full((1024, D_MODEL), **once), full((1, D_MODEL))],
        out_specs=pl.BlockSpec((Q_BLOCK, D_MODEL), lambda i: (i, 0)),
        out_shape=jax.ShapeDtypeStruct((s, D_MODEL), F32),
        scratch_shapes=[pltpu.VMEM((N_HEADS, NEAR_ROWS, Q_BLOCK), F32),
                        pltpu.VMEM((N_KV, 128, LANES_G), BF16),
                        pltpu.VMEM((N_KV, nblk, LANES_G), F32),
                        pltpu.VMEM((nc, LANES_G), F32),
                        pltpu.VMEM((KEY_TILE, LANES_G), F32),
                        pltpu.VMEM((3, N_KV, HEAD_DIM, LANES_G), F32),
                        pltpu.VMEM((N_HEADS * HEAD_DIM, Q_BLOCK), F32)],
        compiler_params=_cparams("arbitrary"),
        name="nsa_attn_prompt",
    )(rel_bias, qt, kck, kcvt, selk, selvt, *([wink] * n_win), *([winvt] * n_win), gt, u, x1, w_out, gfin)


PAGES_PER_STEP = 8
CHUNK_PAGES = 32
STEPS_PER_CHUNK = CHUNK_PAGES // PAGES_PER_STEP
NEAR_PAGES = MAX_DISTANCE // PAGE_SIZE
NEW_PAD = 128


def _proj_sample_kernel(x_ref, g_ref, wn_ref, kvcs_ref, kvw_ref, q_ref, u_ref, gate_ref):
    xn = _rms(x_ref[...], g_ref[...]).astype(BF16)
    z = jnp.dot(xn, wn_ref[...], preferred_element_type=F32)
    kvcs_ref[...] = z[:, 0:1024]
    kvw_ref[...] = z[:, 1024:1536]
    q_ref[...] = (z[:, 1536:2560] * (HEAD_DIM ** -0.5)).astype(BF16)
    u_ref[...] = z[:, 2560:3584]
    gate_ref[...] = _sigmoid(z[:, 3584:3584 + GATE_PAD])


def _proj_sample(x1, g, w_nat):
    rows = x1.shape[0]
    sds = jax.ShapeDtypeStruct
    return pl.pallas_call(
        _proj_sample_kernel,
        out_shape=[sds((rows, 1024), F32), sds((rows, 512), F32), sds((rows, 1024), BF16),
                   sds((rows, 1024), F32), sds((rows, GATE_PAD), F32)],
        compiler_params=pltpu.CompilerParams(vmem_limit_bytes=VMEM_LIMIT),
        name="nsa_proj_sample",
    )(x1, g, w_nat)


def _softmax_lanes_online(s, m_old, l_old):
    m_new = jnp.maximum(m_old, jnp.max(s, axis=1, keepdims=True))
    alpha = jnp.exp(m_old - m_new)
    p = jnp.exp(s - m_new)
    return m_new, alpha, alpha * l_old + jnp.sum(p, axis=1, keepdims=True), p


def _attn_sample_kernel(pt_ref, *refs, past, ds, wb):
    pages = refs[:PAGES_PER_STEP]
    (q_ref, new_ref, wint_ref, gate_ref, u_ref, x1_ref, perm_ref, wk_ref, wv_ref, tb_ref, wout_ref, gfin_ref, y_ref,
     xl_s, kckt_s, kcv_s, skt_s, svt_s, bc_s, bs_s, bnew_s, bw_s, obr_s) = refs[PAGES_PER_STEP:]
    b = pl.program_id(0)
    pg = pl.program_id(1)
    npg = pl.num_programs(1)
    rows = Q_PER_KV * N_KV * ds
    nchunk = past // (CHUNK_PAGES * PAGE_SIZE)
    npages = past // PAGE_SIZE
    ncl = past // CMP_BLOCK
    nblk = past // SEL_BLOCK
    k_top = min(TOP_N - 1, nblk)
    nt = (((1,), (1,)), ((), ()))

    row_i = lax.broadcasted_iota(jnp.int32, (rows, 1), 0)
    tt = row_i % ds

    @pl.when((b == 0) & (pg == 0))
    def _():
        tbs = tb_ref[...] - tb_ref[:, N_BUCKETS - 1:N_BUCKETS]
        col = lambda k: tbs[:, k:k + 1]
        lane = lax.broadcasted_iota(jnp.int32, (1, ncl), 1)
        slot = lane % 8
        cblk = 8 * (lane // 8) + jnp.where(slot < 4, 2 * slot, 2 * (slot - 4) + 1)
        bc_s[...] = _bias_from_dist(past + tt - (CMP_BLOCK * cblk + CMP_BLOCK - 1), col)
        lane_p = lax.broadcasted_iota(jnp.int32, (1, PAGE_SIZE), 1)
        for k in range(NEAR_PAGES):
            bs_s[k] = _bias_from_dist(tt + MAX_DISTANCE - PAGE_SIZE * k - lane_p, col)
        bnew_s[...] = _bias_from_dist(tt - lax.broadcasted_iota(jnp.int32, (1, NEW_PAD), 1), col)
        bw_s[...] = _bias_from_dist(tt + wb - lax.broadcasted_iota(jnp.int32, (1, wb), 1), col)

    for pair in range(PAGES_PER_STEP // 2):
        pa, pb = pages[2 * pair], pages[2 * pair + 1]
        slab = pl.ds(pl.multiple_of(((pg % STEPS_PER_CHUNK) * (PAGES_PER_STEP // 2) + pair) * 8, 8), 8)
        for kind in range(2):
            xt2 = jnp.concatenate([pa[0, kind * 256:(kind + 1) * 256, :], pb[0, kind * 256:(kind + 1) * 256, :]],
                                  axis=1).astype(BF16)
            x_perm = lax.dot_general(perm_ref[...], xt2, nt, preferred_element_type=F32)
            xl_s[:, slab, kind * 256:(kind + 1) * 256] = x_perm.reshape(CMP_BLOCK, 8, 256)
    for k in range(PAGES_PER_STEP):
        skt_s[pg * PAGES_PER_STEP + k] = pages[k][0, 512:768, :].astype(BF16)
        svt_s[pg * PAGES_PER_STEP + k] = pages[k][0, 768:1024, :].astype(BF16)

    @pl.when(pg % STEPS_PER_CHUNK == STEPS_PER_CHUNK - 1)
    def _():
        nbc = CHUNK_PAGES * PAGE_SIZE // CMP_BLOCK

        def body(l, acc):
            ak, av = acc
            xk = xl_s[l, :, 0:256].astype(BF16)
            xv = xl_s[l, :, 256:512].astype(BF16)
            ak = ak + jnp.dot(xk, wk_ref[l], preferred_element_type=F32)
            av = av + jnp.dot(xv, wv_ref[l], preferred_element_type=F32)
            return ak, av

        ak, av = lax.fori_loop(0, CMP_BLOCK, body, (jnp.zeros((nbc, 256), F32), jnp.zeros((nbc, 256), F32)),
                               unroll=4)
        ch = pg // STEPS_PER_CHUNK
        kckt_s[ch] = ak.T.astype(BF16)
        kcv_s[ch] = av.astype(BF16)

    @pl.when(pg == npg - 1)
    def _():
        lane_g = lax.broadcasted_iota(jnp.int32, (1, 256), 1) // HEAD_DIM
        row_g = (row_i // ds) % N_KV
        wq = jnp.where(row_g == lane_g, q_ref[0], jnp.zeros((), BF16))
        t_row = past + tt

        s_c = jnp.concatenate([jnp.dot(wq, kckt_s[ch], preferred_element_type=F32) for ch in range(nchunk)],
                              axis=1) + bc_s[...]
        m = jnp.max(s_c, axis=1, keepdims=True)
        e = jnp.exp(s_c - m)
        p_c = e / jnp.sum(e, axis=1, keepdims=True)
        o_c = jnp.zeros((rows, 256), F32)
        for ch in range(nchunk):
            o_c = o_c + jnp.dot(p_c[:, ch * 128:(ch + 1) * 128].astype(BF16), kcv_s[ch], preferred_element_type=F32)
        obr_s[0] = o_c

        gt_rows = N_KV * ds
        ps = p_c[0:gt_rows]
        for r in range(1, Q_PER_KV):
            ps = ps + p_c[r * gt_rows:(r + 1) * gt_rows]
        imp = ps + pltpu.roll(ps, ncl - 4, axis=1)
        lane = lax.broadcasted_iota(jnp.int32, (gt_rows, ncl), 1)
        slot = lane % 8
        jblk = 4 * (lane // 8) + slot
        t_gt = past + lax.broadcasted_iota(jnp.int32, (gt_rows, ncl), 0) % ds
        forced = (jblk == 0) | (jblk == t_gt // SEL_BLOCK)
        valid = jblk * SEL_BLOCK <= t_gt
        scores = jnp.where(slot < 4, jnp.where(forced, jnp.inf, jnp.where(valid, imp, NEG_INF)), NEG_INF)
        jf = jnp.where(slot < 4, jblk, 2 * nblk).astype(F32)
        picks = []
        for _ in range(k_top):
            mx = jnp.max(scores, axis=1, keepdims=True)
            idx = jnp.min(jnp.where(scores == mx, jf, float(2 * nblk)), axis=1, keepdims=True)
            scores = jnp.where(jf == idx, NEG_INF, scores)
            picks.append(idx)

        gk = NEAR_PAGES
        blk_lane = (lax.broadcasted_iota(jnp.int32, (1, gk * PAGE_SIZE), 1) // SEL_BLOCK).astype(F32)

        def pages_step(pgi, carry, bias):
            m_old, l_old, acc = carry
            s = jnp.concatenate([jnp.dot(wq, skt_s[pgi * gk + k], preferred_element_type=F32) for k in range(gk)],
                                axis=1)
            if bias is not None:
                s = s + bias
            jk = blk_lane + jnp.asarray(pgi * (gk * PAGE_SIZE // SEL_BLOCK), F32)
            hit = jk == picks[0]
            for pk in picks[1:]:
                hit = hit | (jk == pk)
            pen = jnp.where(hit, 0.0, NEG_INF)
            s = (s.reshape(Q_PER_KV, gt_rows, gk * PAGE_SIZE) + pen[None]).reshape(rows, gk * PAGE_SIZE)
            m_new, alpha, l_new, pr = _softmax_lanes_online(s, m_old, l_old)
            prb = pr.astype(BF16)
            pv = alpha * acc
            for k in range(gk):
                pv = pv + lax.dot_general(prb[:, k * PAGE_SIZE:(k + 1) * PAGE_SIZE], svt_s[pgi * gk + k], nt,
                                          preferred_element_type=F32)
            return m_new, l_new, pv

        carry = (jnp.full((rows, 1), M_FLOOR, F32), jnp.zeros((rows, 1), F32), jnp.zeros((rows, 256), F32))
        carry = lax.fori_loop(0, npages // gk - 1, lambda j, c: pages_step(j, c, None), carry)
        bias_near = jnp.concatenate([bs_s[k] for k in range(NEAR_PAGES)], axis=1)
        m_old, l_old, acc = pages_step(npages // gk - 1, carry, bias_near)
        lane_n = lax.broadcasted_iota(jnp.int32, (1, NEW_PAD), 1)
        ok_new = (lane_n < ds) & (lane_n <= tt)
        s = lax.dot_general(wq, new_ref[0, 0], nt, preferred_element_type=F32) + bnew_s[...]
        m_new, alpha, l_new, pr = _softmax_lanes_online(jnp.where(ok_new, s, NEG_INF), m_old, l_old)
        acc = alpha * acc + jnp.dot(pr.astype(BF16), new_ref[0, 1], preferred_element_type=F32)
        obr_s[1] = acc / l_new

        kwt = wint_ref[0, 0:256, :].astype(BF16)
        vwt = wint_ref[0, 256:512, :].astype(BF16)
        dist = tt + wb - lax.broadcasted_iota(jnp.int32, (1, wb), 1)
        s_w = jnp.where((dist >= 0) & (dist < WINDOW), jnp.dot(wq, kwt, preferred_element_type=F32) + bw_s[...], NEG_INF)
        s_n = jnp.where(ok_new, lax.dot_general(wq, new_ref[0, 2], nt, preferred_element_type=F32) + bnew_s[...], NEG_INF)
        m = jnp.maximum(jnp.max(s_w, axis=1, keepdims=True), jnp.max(s_n, axis=1, keepdims=True))
        e_w = jnp.exp(s_w - m)
        e_n = jnp.exp(s_n - m)
        den = jnp.sum(e_w, axis=1, keepdims=True) + jnp.sum(e_n, axis=1, keepdims=True)
        o_w = (lax.dot_general(e_w.astype(BF16), vwt, nt, preferred_element_type=F32)
               + jnp.dot(e_n.astype(BF16), new_ref[0, 3], preferred_element_type=F32))
        obr_s[2] = o_w / den

        merged = []
        for r in range(Q_PER_KV):
            tot = None
            for br in range(3):
                o_r = jnp.zeros((ds, 256), F32)
                g_r = jnp.zeros((ds, 256), F32)
                for g in range(N_KV):
                    base = r * gt_rows + g * ds
                    o_r = jnp.where(lane_g == g, obr_s[br, base:base + ds, :], o_r)
                    c = br * N_HEADS + g * Q_PER_KV + r
                    g_r = jnp.where(lane_g == g, gate_ref[0, :, c:c + 1], g_r)
                tot = g_r * o_r if tot is None else tot + g_r * o_r
            merged.append(tot)
        op = jnp.concatenate(merged, axis=1)
        y = jnp.dot((op * _silu(u_ref[0])).astype(BF16), wout_ref[...], preferred_element_type=F32)
        y_ref[0] = _rms(x1_ref[0] + y, gfin_ref[...])


def _attn_sample(page_table, pool_t, q64, newkv, win_t, gates, u_r, x1, perm, wkt_bd, wv_bd, tb_rows, w_out_r, gfin,
                 *, past, ds, wb):
    nb = page_table.shape[0]
    npages = past // PAGE_SIZE
    assert past % (CHUNK_PAGES * PAGE_SIZE) == 0 and npages >= NEAR_PAGES and past // SEL_BLOCK >= TOP_N
    npg = npages // PAGES_PER_STEP
    nchunk = npages // CHUNK_PAGES
    rows = Q_PER_KV * N_KV * ds
    ncl = past // CMP_BLOCK
    once = dict(pipeline_mode=pl.Buffered(1))
    full = lambda shp, **kw: pl.BlockSpec(shp, lambda b, p, pt: (0,) * len(shp), **kw)
    per_b = lambda shp: pl.BlockSpec((1,) + shp, lambda b, p, pt: (b,) + (0,) * len(shp))
    page_specs = [pl.BlockSpec((1, 1024, PAGE_SIZE),
                               functools.partial(lambda b, p, pt, k: (pt[b, p * PAGES_PER_STEP + k], 0, 0), k=k))
                  for k in range(PAGES_PER_STEP)]
    grid_spec = pltpu.PrefetchScalarGridSpec(
        num_scalar_prefetch=1, grid=(nb, npg),
        in_specs=[*page_specs, per_b((rows, 256)), per_b((4, NEW_PAD, 256)), per_b((512, wb)),
                  per_b((ds, GATE_PAD)), per_b((ds, 1024)), per_b((ds, D_MODEL)),
                  full((256, 256), **once), full((CMP_BLOCK, 256, 256), **once), full((CMP_BLOCK, 256, 256), **once),
                  full((rows, N_BUCKETS)), full((1024, D_MODEL), **once), full((1, D_MODEL))],
        out_specs=pl.BlockSpec((1, ds, D_MODEL), lambda b, p, pt: (b, 0, 0)),
        scratch_shapes=[pltpu.VMEM((CMP_BLOCK, CHUNK_PAGES * PAGE_SIZE // CMP_BLOCK, 512), F32),
                        pltpu.VMEM((nchunk, 256, 128), BF16), pltpu.VMEM((nchunk, 128, 256), BF16),
                        pltpu.VMEM((npages, 256, PAGE_SIZE), BF16), pltpu.VMEM((npages, 256, PAGE_SIZE), BF16),
                        pltpu.VMEM((rows, ncl), F32), pltpu.VMEM((NEAR_PAGES, rows, PAGE_SIZE), F32),
                        pltpu.VMEM((rows, NEW_PAD), F32), pltpu.VMEM((rows, wb), F32),
                        pltpu.VMEM((3, rows, 256), F32)])
    return pl.pallas_call(
        functools.partial(_attn_sample_kernel, past=past, ds=ds, wb=wb),
        grid_spec=grid_spec,
        out_shape=jax.ShapeDtypeStruct((nb, ds, D_MODEL), F32),
        compiler_params=_cparams("arbitrary", "arbitrary"),
        name="nsa_attn_sample",
    )(page_table, *([pool_t] * PAGES_PER_STEP), q64, newkv, win_t, gates, u_r, x1, perm, wkt_bd, wv_bd, tb_rows,
      w_out_r, gfin)


def _pair_permutation():
    m = jnp.arange(256)
    l, slot = m // 8, m % 8
    cb = jnp.where(slot < 4, 2 * slot, 2 * (slot - 4) + 1)
    tok = CMP_BLOCK * cb + l
    return (tok[:, None] == jnp.arange(256)[None, :]).astype(BF16)


def _block_diag4(w):
    eye = jnp.eye(N_KV, dtype=w.dtype)
    out = eye[:, None, :, None] * w[..., None, :, None, :]
    return out.reshape(*w.shape[:-2], N_KV * HEAD_DIM, N_KV * HEAD_DIM)


def kernel(x_prompt, x_sample, state_lru_h, state_conv, cache_nsa_kv, cache_win_kv, page_table, norm_g,
           final_norm_g, w_in_lru, conv_w, conv_b, w_gate_a, b_gate_a, w_gate_x, b_gate_x, lru_lambda,
           w_out_lru, w_in_nsa, w_cmp, w_out_nsa, rel_bias):
    bp, s_len, _ = x_prompt.shape
    nb, ds, _ = x_sample.shape
    assert bp == 1
    row = lambda v: v.reshape(1, -1)

    lru_w = (row(norm_g[0]), w_in_lru[0].astype(BF16), conv_w[0], row(conv_b[0]), w_gate_a[0].astype(BF16),
             row(b_gate_a[0]), w_gate_x[0].astype(BF16), row(b_gate_x[0]), row(lru_lambda[0]),
             w_out_lru[0].astype(BF16))
    x1p, hp, tailp = _lru_prompt(x_prompt[0], *lru_w)
    xs_tm = x_sample.transpose(1, 0, 2).reshape(ds * nb, D_MODEL)
    buf_tm = state_conv[0].transpose(1, 0, 2).reshape((CONV_W - 1) * nb, D_MODEL)
    x1s_tm, hs_last, tails = _lru_sample(xs_tm, buf_tm, state_lru_h[0], *lru_w, nb=nb, ds=ds)

    prompt_lru_h = hp.reshape(1, 1, D_MODEL)
    prompt_conv = tailp[8 - (CONV_W - 1):].reshape(1, 1, CONV_W - 1, D_MODEL)
    sample_lru_h = hs_last.reshape(1, nb, D_MODEL)
    sample_conv = tails.reshape(CONV_W - 1, nb, D_MODEL).transpose(1, 0, 2)[None]

    w = w_in_nsa[0]
    wq, wkv = w[:, :1024], w[:, 1024:1024 + KV_WIDTH]
    wg, wu = w[:, 1024 + KV_WIDTH:1024 + KV_WIDTH + 3 * N_HEADS], w[:, 1024 + KV_WIDTH + 3 * N_HEADS:]
    wg_pad = jnp.pad(wg, ((0, 0), (0, GATE_PAD - 3 * N_HEADS)))
    w_nat_p = jnp.concatenate([wkv[:, 0:768], wkv[:, 1024:1280], wu], axis=1).astype(BF16)
    w_t_p = jnp.concatenate([wq.T, wkv.T, wg_pad.T], axis=0).astype(BF16)
    wk_bd = _block_diag4(w_cmp[0, 0]).astype(BF16)
    wvt_bd = jnp.swapaxes(_block_diag4(w_cmp[0, 1]), -1, -2).astype(BF16)
    w_out = w_out_nsa[0].astype(BF16)
    g1 = row(norm_g[1])
    gfin = row(final_norm_g)

    cmp_kv, selk, wink, u_p, qt, kvcst, kvwt, selvt, winvt, gt = _proj_prompt(x1p, g1, w_nat_p, w_t_p)
    kck, kcvt = _compress_prompt(cmp_kv, wk_bd, wvt_bd)
    y_prompt = _attn_prompt(rel_bias, qt, kck, kcvt, selk, selvt, wink, winvt, gt, u_p, x1p, w_out, gfin)
    y_prompt = y_prompt[None]
    prompt_kv_rows = kvcst.reshape(4, N_KV, HEAD_DIM, s_len).transpose(3, 0, 1, 2)[None, None]
    wbp = min(WINDOW, s_len)
    prompt_win_kv = kvwt[:, s_len - wbp:].reshape(2, N_KV, HEAD_DIM, wbp).transpose(3, 0, 1, 2)[None, None]

    n_phys, n_layers = cache_nsa_kv.shape[:2]
    past = page_table.shape[1] * PAGE_SIZE
    wb = cache_win_kv.shape[2]
    to_rgd = lambda m: m.reshape(m.shape[0], N_KV, Q_PER_KV, HEAD_DIM).transpose(0, 2, 1, 3).reshape(m.shape[0], 1024)
    w_nat_s = jnp.concatenate([wkv, to_rgd(wq), to_rgd(wu), wg_pad], axis=1).astype(BF16)
    w_out_r = to_rgd(w_out_nsa[0].T).T.astype(BF16)
    x1s = x1s_tm.reshape(ds, nb, D_MODEL).transpose(1, 0, 2).reshape(nb * ds, D_MODEL)
    kvcs_s, kvw_s, q_s, u_s, gate_s = _proj_sample(x1s, g1, w_nat_s)
    q64 = q_s.reshape(nb, ds, Q_PER_KV, 1, 256).transpose(0, 2, 3, 1, 4)
    q64 = jnp.broadcast_to(q64, (nb, Q_PER_KV, N_KV, ds, 256)).reshape(nb, Q_PER_KV * N_KV * ds, 256)
    new4 = jnp.stack([kvcs_s[:, 512:768], kvcs_s[:, 768:1024], kvw_s[:, 0:256], kvw_s[:, 256:512]], axis=0)
    new4 = new4.reshape(4, nb, ds, 256).transpose(1, 0, 2, 3).astype(BF16)
    newkv = jnp.pad(new4, ((0, 0), (0, 0), (0, NEW_PAD - ds), (0, 0)))
    pool_t = cache_nsa_kv.transpose(0, 1, 3, 4, 5, 2).reshape(n_phys * n_layers, 1024, PAGE_SIZE)
    win_t = cache_win_kv[0].transpose(0, 2, 3, 4, 1).reshape(nb, 512, wb)
    head_of_row = (jnp.arange(Q_PER_KV * N_KV * ds) // ds % N_KV) * Q_PER_KV + jnp.arange(Q_PER_KV * N_KV * ds) // (N_KV * ds)
    tb_rows = rel_bias.T[head_of_row]
    y_sample = _attn_sample(page_table * n_layers, pool_t, q64, newkv, win_t, gate_s.reshape(nb, ds, GATE_PAD),
                            u_s.reshape(nb, ds, 1024), x1s.reshape(nb, ds, D_MODEL), _pair_permutation(),
                            wk_bd, jnp.swapaxes(wvt_bd, -1, -2), tb_rows, w_out_r, gfin,
                            past=past, ds=ds, wb=wb)
    sample_kv_rows = kvcs_s.reshape(nb, 1, ds, 4, N_KV, HEAD_DIM)
    new_t = kvw_s.reshape(nb, ds, 512).transpose(0, 2, 1)
    win_all = jnp.concatenate([win_t, new_t], axis=2)[:, :, ds:]
    sample_win_kv = win_all.reshape(nb, 2, N_KV, HEAD_DIM, wb).transpose(0, 4, 1, 2, 3)[None]

    return (y_prompt, y_sample, prompt_lru_h, prompt_conv, prompt_kv_rows, prompt_win_kv,
            sample_lru_h, sample_conv, sample_kv_rows, sample_win_kv)
```

```python
import functools
import math

import jax
import jax.numpy as jnp
from jax import lax
from jax.experimental import pallas as pl
from jax.experimental.pallas import tpu as pltpu

F32 = jnp.float32
BF16 = jnp.bfloat16

D_MODEL = 1024
HEAD_DIM = 64
N_HEADS = 16
N_KV = 4
Q_PER_KV = 4
CMP_BLOCK = 32
SEL_BLOCK = 64
TOP_N = 16
WINDOW = 512
PAGE_SIZE = 128
N_BUCKETS = 32
MAX_DISTANCE = 1024
N_LRU_BLOCKS = 8
LRU_BLOCK = 128
CONV_W = 4
LRU_C = 8.0
RMS_EPS = 1e-6

V7X_VMEM_BYTES = 64 * 1024 * 1024
VMEM_LIMIT = V7X_VMEM_BYTES - 8 * 1024 * 1024


def _bucket_lower_bounds():
    max_exact = N_BUCKETS // 2
    lows = list(range(1, max_exact + 1))
    for k in range(1, N_BUCKETS - max_exact):
        n = max_exact
        while n ** 8 < 2 ** (32 + 3 * k):
            n += 1
        lows.append(n)
    return lows


_BUCKET_LOWS = _bucket_lower_bounds()


def _bias_from_dist(dist, table_col):
    val = jnp.full(dist.shape, table_col(0), F32)
    for b, lo in enumerate(_BUCKET_LOWS, start=1):
        val = jnp.where(dist >= lo, table_col(b), val)
    return val


def _rms(x, g):
    ms = jnp.mean(x * x, axis=-1, keepdims=True)
    return x * lax.rsqrt(ms + RMS_EPS) * g


def _sigmoid(x):
    return 1.0 / (1.0 + jnp.exp(-x))


def _silu(x):
    return x * _sigmoid(x)


def _softplus(x):
    return jnp.maximum(x, 0.0) + jnp.log1p(jnp.exp(-jnp.abs(x)))


def _expm1(x):
    u = jnp.exp(x)
    um1 = u - 1.0
    y = um1 * x / jnp.log(u)
    y = jnp.where(u == 1.0, x, y)
    return jnp.where(um1 == -1.0, -1.0, y)


def _cparams(*sem):
    return pltpu.CompilerParams(dimension_semantics=sem, vmem_limit_bytes=VMEM_LIMIT)


def _lru_gate_terms(conv, wa_ref, ba, wx_ref, bx, lam):
    cb = conv.astype(BF16)
    r_parts, i_parts = [], []
    for n in range(N_LRU_BLOCKS):
        c = cb[:, n * LRU_BLOCK:(n + 1) * LRU_BLOCK]
        r_parts.append(jnp.dot(c, wa_ref[n], preferred_element_type=F32))
        i_parts.append(jnp.dot(c, wx_ref[n], preferred_element_type=F32))
    r = _sigmoid(jnp.concatenate(r_parts, axis=1) + ba)
    i = _sigmoid(jnp.concatenate(i_parts, axis=1) + bx)
    log_a = -LRU_C * r * _softplus(-lam)
    a = jnp.exp(log_a)
    u = jnp.sqrt(-_expm1(2.0 * log_a)) * i * conv
    return a, u


def _lru_prompt_kernel(x_ref, g_ref, win_ref, cw_ref, cb_ref, wa_ref, ba_ref, wx_ref, bx_ref,
                       lam_ref, wout_ref, x1_ref, hlast_ref, tail_ref,
                       xb_ext, a_s, u_s, h_c, *, tr):
    i = pl.program_id(0)

    @pl.when(i == 0)
    def _():
        xb_ext[0:8, :] = jnp.zeros((8, D_MODEL), F32)
        h_c[...] = jnp.zeros_like(h_c)

    x = x_ref[...]
    xn = _rms(x, g_ref[...])
    z = jnp.dot(xn.astype(BF16), win_ref[...], preferred_element_type=F32)
    xb = z[:, :D_MODEL]
    gate = z[:, D_MODEL:]
    xb_ext[8:8 + tr, :] = xb
    conv = cb_ref[...]
    for j in range(CONV_W):
        conv = conv + cw_ref[j:j + 1, :] * xb_ext[8 - (CONV_W - 1) + j:8 - (CONV_W - 1) + j + tr, :]
    a, u = _lru_gate_terms(conv, wa_ref, ba_ref[...], wx_ref, bx_ref[...], lam_ref[...])
    a_s[...] = a
    u_s[...] = u

    row = lax.broadcasted_iota(jnp.int32, (8, D_MODEL), 0)

    def body(k, h):
        base = pl.multiple_of(k * 8, 8)
        aa = a_s[pl.ds(base, 8), :]
        uu = u_s[pl.ds(base, 8), :]
        for s in (1, 2, 4):
            keep = row >= s
            us = jnp.where(keep, pltpu.roll(uu, s, axis=0), 0.0)
            as_ = jnp.where(keep, pltpu.roll(aa, s, axis=0), 1.0)
            uu = uu + aa * us
            aa = aa * as_
        hb = aa * h + uu
        u_s[pl.ds(base, 8), :] = hb
        return hb[7:8, :]

    h_last = lax.fori_loop(0, tr // 8, body, h_c[...])
    h_c[...] = h_last
    hlast_ref[...] = h_last
    tail_ref[...] = xb[tr - 8:, :]
    xb_ext[0:8, :] = xb[tr - 8:, :]
    y = u_s[...] * _silu(gate)
    out = jnp.dot(y.astype(BF16), wout_ref[...], preferred_element_type=F32)
    x1_ref[...] = x + out


def _lru_prompt(x, g, w_in, cw, cb, wa, ba, wx, bx, lam, w_out, tr=512):
    s = x.shape[0]
    tr = min(tr, s)
    assert s % tr == 0 and tr % 8 == 0
    full = lambda shp: pl.BlockSpec(shp, lambda i: (0,) * len(shp))
    return pl.pallas_call(
        functools.partial(_lru_prompt_kernel, tr=tr),
        grid=(s // tr,),
        in_specs=[pl.BlockSpec((tr, D_MODEL), lambda i: (i, 0)), full((1, D_MODEL)),
                  full((D_MODEL, 2 * D_MODEL)), full((CONV_W, D_MODEL)), full((1, D_MODEL)),
                  full((N_LRU_BLOCKS, LRU_BLOCK, LRU_BLOCK)), full((1, D_MODEL)),
                  full((N_LRU_BLOCKS, LRU_BLOCK, LRU_BLOCK)), full((1, D_MODEL)),
                  full((1, D_MODEL)), full((D_MODEL, D_MODEL))],
        out_specs=[pl.BlockSpec((tr, D_MODEL), lambda i: (i, 0)), full((1, D_MODEL)), full((8, D_MODEL))],
        out_shape=[jax.ShapeDtypeStruct((s, D_MODEL), F32), jax.ShapeDtypeStruct((1, D_MODEL), F32),
                   jax.ShapeDtypeStruct((8, D_MODEL), F32)],
        scratch_shapes=[pltpu.VMEM((tr + 8, D_MODEL), F32), pltpu.VMEM((tr, D_MODEL), F32),
                        pltpu.VMEM((tr, D_MODEL), F32), pltpu.VMEM((1, D_MODEL), F32)],
        compiler_params=_cparams("arbitrary"),
        name="lru_prompt",
    )(x, g, w_in, cw, cb, wa, ba, wx, bx, lam, w_out)


def _lru_sample_kernel(x_ref, buf_ref, h0_ref, g_ref, win_ref, cw_ref, cb_ref, wa_ref, ba_ref, wx_ref,
                       bx_ref, lam_ref, wout_ref, x1_ref, hlast_ref, tail_ref, xp_s, *, nb, ds):
    x = x_ref[...]
    xn = _rms(x, g_ref[...])
    z = jnp.dot(xn.astype(BF16), win_ref[...], preferred_element_type=F32)
    xb = z[:, :D_MODEL]
    gate = z[:, D_MODEL:]
    nbuf = (CONV_W - 1) * nb
    xp_s[0:nbuf, :] = buf_ref[...]
    xp_s[nbuf:, :] = xb
    conv = cb_ref[...]
    for j in range(CONV_W):
        conv = conv + cw_ref[j:j + 1, :] * xp_s[j * nb:j * nb + ds * nb, :]
    a, u = _lru_gate_terms(conv, wa_ref, ba_ref[...], wx_ref, bx_ref[...], lam_ref[...])
    h = h0_ref[...]
    hs = []
    for t in range(ds):
        h = a[t * nb:(t + 1) * nb, :] * h + u[t * nb:(t + 1) * nb, :]
        hs.append(h)
    hlast_ref[...] = h
    tail_ref[...] = xp_s[ds * nb:, :]
    y = jnp.concatenate(hs, axis=0) * _silu(gate)
    out = jnp.dot(y.astype(BF16), wout_ref[...], preferred_element_type=F32)
    x1_ref[...] = x + out


def _lru_sample(x_tm, buf_tm, h0, g, w_in, cw, cb, wa, ba, wx, bx, lam, w_out, nb, ds):
    rows = nb * ds
    nbuf = (CONV_W - 1) * nb
    return pl.pallas_call(
        functools.partial(_lru_sample_kernel, nb=nb, ds=ds),
        out_shape=[jax.ShapeDtypeStruct((rows, D_MODEL), F32), jax.ShapeDtypeStruct((nb, D_MODEL), F32),
                   jax.ShapeDtypeStruct((nbuf, D_MODEL), F32)],
        scratch_shapes=[pltpu.VMEM((nbuf + rows, D_MODEL), F32)],
        compiler_params=pltpu.CompilerParams(vmem_limit_bytes=VMEM_LIMIT),
        name="lru_sample",
    )(x_tm, buf_tm, h0, g, w_in, cw, cb, wa, ba, wx, bx, lam, w_out)


Q_BLOCK = 128
LANES_G = Q_PER_KV * Q_BLOCK
KEY_TILE = 512
NEAR_PAD = KEY_TILE
NEAR_ROWS = NEAR_PAD + MAX_DISTANCE + KEY_TILE
NEG_INF = float("-inf")
LOG2E = math.log2(math.e)
M_FLOOR = -1e30

KV_WIDTH = 3 * 2 * N_KV * HEAD_DIM
GATE_PAD = 128
V_ROWS = HEAD_DIM + 16


def _proj_prompt_kernel(x_ref, g_ref, wn_ref, wt_ref, cmp_ref, selk_ref, wink_ref, u_ref,
                        qt_ref, kvcst_ref, kvwt_ref, selvt_ref, winvt_ref, gt_ref, *, tr):
    xn = _rms(x_ref[...], g_ref[...]).astype(BF16)
    z = jnp.dot(xn, wn_ref[...], preferred_element_type=F32)
    cmp_ref[...] = z[:, 0:512].astype(BF16)
    for p in range(2):
        selk_ref[p] = z[:, 512 + 128 * p:640 + 128 * p].astype(BF16)
        wink_ref[p] = z[:, 768 + 128 * p:896 + 128 * p].astype(BF16)
    u_ref[...] = z[:, 1024:2048]
    zt = lax.dot_general(wt_ref[...], xn, (((1,), (1,)), ((), ())), preferred_element_type=F32)
    qt_ref[...] = (zt[0:1024] * (HEAD_DIM ** -0.5 * LOG2E)).astype(BF16).reshape(N_KV, Q_PER_KV * HEAD_DIM, tr)
    kvt = zt[1024:1024 + KV_WIDTH]
    kvcst_ref[...] = kvt[0:1024]
    kvwt_ref[...] = kvt[1024:1536]
    selvt_ref[:, 0:HEAD_DIM, :] = kvt[768:1024].astype(BF16).reshape(N_KV, HEAD_DIM, tr)
    pad_row = lax.broadcasted_iota(jnp.int32, (N_KV, V_ROWS - HEAD_DIM, tr), 1)
    selvt_ref[:, HEAD_DIM:V_ROWS, :] = jnp.where(pad_row == 0, 1.0, 0.0).astype(BF16)
    winvt_ref[...] = kvt[1280:1536].astype(BF16).reshape(N_KV, HEAD_DIM, tr)
    gt_ref[...] = _sigmoid(zt[1024 + KV_WIDTH:1024 + KV_WIDTH + GATE_PAD])


def _proj_prompt(x1, g, w_nat, w_t, tr=512):
    s = x1.shape[0]
    tr = min(tr, s)
    nn, nt = w_nat.shape[1], w_t.shape[0]
    full = lambda shp: pl.BlockSpec(shp, lambda i: (0,) * len(shp))
    sds = jax.ShapeDtypeStruct
    return pl.pallas_call(
        functools.partial(_proj_prompt_kernel, tr=tr),
        grid=(s // tr,),
        in_specs=[pl.BlockSpec((tr, D_MODEL), lambda i: (i, 0)), full((1, D_MODEL)), full((D_MODEL, nn)),
                  full((nt, D_MODEL))],
        out_specs=[pl.BlockSpec((tr, 512), lambda i: (i, 0)),
                   pl.BlockSpec((2, tr, 128), lambda i: (0, i, 0)), pl.BlockSpec((2, tr, 128), lambda i: (0, i, 0)),
                   pl.BlockSpec((tr, 1024), lambda i: (i, 0)),
                   pl.BlockSpec((N_KV, 256, tr), lambda i: (0, 0, i)),
                   pl.BlockSpec((1024, tr), lambda i: (0, i)), pl.BlockSpec((512, tr), lambda i: (0, i)),
                   pl.BlockSpec((N_KV, V_ROWS, tr), lambda i: (0, 0, i)),
                   pl.BlockSpec((N_KV, HEAD_DIM, tr), lambda i: (0, 0, i)),
                   pl.BlockSpec((GATE_PAD, tr), lambda i: (0, i))],
        out_shape=[sds((s, 512), BF16), sds((2, s, 128), BF16), sds((2, s, 128), BF16),
                   sds((s, 1024), F32), sds((N_KV, 256, s), BF16), sds((1024, s), F32), sds((512, s), F32),
                   sds((N_KV, V_ROWS, s), BF16), sds((N_KV, HEAD_DIM, s), BF16), sds((GATE_PAD, s), F32)],
        compiler_params=_cparams("arbitrary"),
        name="nsa_proj_prompt",
    )(x1, g, w_nat, w_t)


def _compress_prompt_kernel(xe_ref, xo_ref, wk_ref, wvt_ref, kck_ref, kcvt_ref, acck, accv, *, tl, nh):
    step = pl.program_id(0)

    @pl.when(step == 0)
    def _():
        acck[...] = jnp.zeros_like(acck)
        accv[...] = jnp.zeros_like(accv)

    for half, x_ref in enumerate((xe_ref, xo_ref)):
        for l in range(tl):
            xk = x_ref[:, l * 512:l * 512 + 256]
            xv = x_ref[:, l * 512 + 256:l * 512 + 512]
            acck[half * nh:(half + 1) * nh, :] += jnp.dot(xk, wk_ref[l], preferred_element_type=F32)
            accv[:, half * nh:(half + 1) * nh] += lax.dot_general(
                wvt_ref[l], xv, (((1,), (1,)), ((), ())), preferred_element_type=F32)

    @pl.when(step == pl.num_programs(0) - 1)
    def _():
        for p in range(2):
            kck_ref[p] = acck[:, 128 * p:128 * (p + 1)].astype(BF16)
        kcvt_ref[...] = accv[...].astype(BF16).reshape(N_KV, HEAD_DIM, 2 * nh)


def _compress_prompt(cmp_kv, wk_bd, wvt_bd, tl=4):
    s = cmp_kv.shape[0]
    nh = s // (2 * CMP_BLOCK)
    x2 = cmp_kv.reshape(nh, 2 * CMP_BLOCK * 512)
    nsteps = CMP_BLOCK // tl
    return pl.pallas_call(
        functools.partial(_compress_prompt_kernel, tl=tl, nh=nh),
        grid=(nsteps,),
        in_specs=[pl.BlockSpec((nh, tl * 512), lambda i: (0, i)),
                  pl.BlockSpec((nh, tl * 512), lambda i: (0, nsteps + i)),
                  pl.BlockSpec((tl, 256, 256), lambda i: (i, 0, 0)),
                  pl.BlockSpec((tl, 256, 256), lambda i: (i, 0, 0))],
        out_specs=[pl.BlockSpec((2, 2 * nh, 128), lambda i: (0, 0, 0)),
                   pl.BlockSpec((N_KV, HEAD_DIM, 2 * nh), lambda i: (0, 0, 0))],
        out_shape=[jax.ShapeDtypeStruct((2, 2 * nh, 128), BF16),
                   jax.ShapeDtypeStruct((N_KV, HEAD_DIM, 2 * nh), BF16)],
        scratch_shapes=[pltpu.VMEM((2 * nh, 256), F32), pltpu.VMEM((256, 2 * nh), F32)],
        compiler_params=_cparams("arbitrary"),
        name="nsa_compress_prompt",
    )(x2, x2, wk_bd, wvt_bd)


def _softmax_rows(s):
    m = jnp.max(s, axis=0, keepdims=True)
    m = jnp.where(m == NEG_INF, 0.0, m)
    e = jnp.exp2(s - m)
    den = jnp.sum(e, axis=0, keepdims=True)
    return e / jnp.where(den > 0, den, 1.0)


def _attn_prompt_kernel(bias_ref, tblt_ref, qt_ref, kck_ref, kcvt_ref, selk_ref, selvt_ref,
                        wk0, wk1, wk2, wk3, wk4, wv0, wv1, wv2, wv3, wv4,
                        gt_ref, u_ref, x1_ref, wout_ref, gfin_ref, y_ref,
                        nb_s, wpen_s, tbl_s, wq_s, pen_s, sc_s, s0_s, obr_s, opt_s, *, s_len):
    i = pl.program_id(0)
    q0 = i * Q_BLOCK
    nc = s_len // CMP_BLOCK
    nch = nc // 2
    nblk = s_len // SEL_BLOCK
    k_top = min(TOP_N, nblk)
    wrows = min(32, nch)
    wk_refs = (wk0, wk1, wk2, wk3, wk4)
    wv_refs = (wv0, wv1, wv2, wv3, wv4)
    n_win = len(wk_refs)

    def table(h):
        return lambda b: (bias_ref[b, h] - bias_ref[N_BUCKETS - 1, h]) * LOG2E

    @pl.when(i == 0)
    def _():
        wq_s[...] = jnp.zeros_like(wq_s)
        rho = lax.broadcasted_iota(jnp.int32, (NEAR_ROWS, Q_BLOCK), 0)
        a = lax.broadcasted_iota(jnp.int32, (NEAR_ROWS, Q_BLOCK), 1)
        dist = a + (MAX_DISTANCE + NEAR_PAD) - rho
        for h in range(N_HEADS):
            nb_s[h] = _bias_from_dist(dist, table(h))
        loc_w = lax.broadcasted_iota(jnp.int32, (n_win * Q_BLOCK, Q_BLOCK), 0)
        d_w = lax.broadcasted_iota(jnp.int32, (n_win * Q_BLOCK, Q_BLOCK), 1) + WINDOW - loc_w
        wpen_s[...] = jnp.where((d_w >= 0) & (d_w < WINDOW), 0.0, NEG_INF)
        t = tblt_ref[...]
        ts = (t - t[:, N_BUCKETS - 1:N_BUCKETS]) * LOG2E
        for h in range(N_HEADS):
            tbl_s[h] = jnp.broadcast_to(ts[h:h + 1, :], (8, 128))

    for g in range(N_KV):
        off = (g % 2) * HEAD_DIM
        for r in range(Q_PER_KV):
            wq_s[g, off:off + HEAD_DIM, r * Q_BLOCK:(r + 1) * Q_BLOCK] = qt_ref[g, r * HEAD_DIM:(r + 1) * HEAD_DIM, :]

    lane_q = lax.broadcasted_iota(jnp.int32, (1, LANES_G), 1) % Q_BLOCK
    t_lane = q0 + lane_q
    blk = lax.broadcasted_iota(jnp.int32, (nblk, Q_BLOCK), 0)
    tq = lax.broadcasted_iota(jnp.int32, (nblk, Q_BLOCK), 1) + q0
    forced = (blk == 0) | (blk == tq // SEL_BLOCK)
    valid = blk * SEL_BLOCK <= tq
    blk_f = blk.astype(F32)

    w0 = jnp.clip(((2 * i - 16) // 8) * 8, 0, nch - wrows)
    w0 = pl.multiple_of(w0, 8)
    jj = lax.broadcasted_iota(jnp.int32, (wrows, Q_BLOCK), 0) + w0
    aq = lax.broadcasted_iota(jnp.int32, (wrows, Q_BLOCK), 1) + q0
    buckets = []
    for half in range(2):
        dist = aq - SEL_BLOCK * jj - (CMP_BLOCK * half + CMP_BLOCK - 1)
        bucket = jnp.zeros((wrows, Q_BLOCK), jnp.int32)
        for lo in _BUCKET_LOWS:
            bucket = bucket + (dist >= lo).astype(jnp.int32)
        buckets.append(bucket)

    def group_body(g, carry):
        wq = wq_s[g]
        pair = g // 2

        sc_s[...] = jnp.dot(kck_ref[pair], wq, preferred_element_type=F32)
        for r in range(Q_PER_KV):
            trow = jnp.concatenate([tbl_s[g * Q_PER_KV + r]] * (wrows // 8), axis=0)
            for half in range(2):
                rows = pl.ds(pl.multiple_of(half * nch + w0, 8), wrows)
                sc_s[rows, r * Q_BLOCK:(r + 1) * Q_BLOCK] += jnp.take_along_axis(trow, buckets[half], axis=1)
        rho = lax.broadcasted_iota(jnp.int32, (nc, 1), 0)
        ends = jnp.where(rho >= nch, SEL_BLOCK * (rho - nch) + 2 * CMP_BLOCK - 1, SEL_BLOCK * rho + CMP_BLOCK - 1)
        p_c = _softmax_rows(jnp.where(ends <= t_lane, sc_s[...], NEG_INF))
        obr_s[0, g] = jnp.dot(kcvt_ref[g], p_c.astype(BF16), preferred_element_type=F32)

        psum = p_c[:, 0:Q_BLOCK]
        for r in range(1, Q_PER_KV):
            psum = psum + p_c[:, r * Q_BLOCK:(r + 1) * Q_BLOCK]
        imp = psum[0:nch] + psum[nch:nc]
        pen_s[g, :, 0:Q_BLOCK] = jnp.where(forced, jnp.inf, jnp.where(valid, imp, NEG_INF))

        kw = jnp.concatenate([wk_refs[k][pair] for k in range(n_win)], axis=0)
        s = jnp.dot(kw, wq, preferred_element_type=F32)
        nw = n_win * Q_BLOCK
        r_w = NEAR_PAD + MAX_DISTANCE - WINDOW
        band = jnp.concatenate(
            [wpen_s[k * Q_BLOCK:(k + 1) * Q_BLOCK, :] + jnp.where(i - (n_win - 1) + k >= 0, 0.0, NEG_INF)
             for k in range(n_win)], axis=0)
        bias = jnp.concatenate([nb_s[g * Q_PER_KV + r, r_w:r_w + nw, :] + band for r in range(Q_PER_KV)], axis=1)
        p_w = _softmax_rows(s + bias)
        vw = jnp.concatenate([wv_refs[k][g] for k in range(n_win)], axis=1)
        obr_s[2, g] = jnp.dot(vw, p_w.astype(BF16), preferred_element_type=F32)
        return carry

    lax.fori_loop(0, N_KV, group_body, 0)

    def pick(_, c):
        for g in range(N_KV):
            sc = pen_s[g, :, 0:Q_BLOCK]
            m = jnp.max(sc, axis=0, keepdims=True)
            idx = jnp.min(jnp.where(sc == m, blk_f, float(nblk)), axis=0, keepdims=True)
            pen_s[g, :, 0:Q_BLOCK] = jnp.where(blk_f == idx, NEG_INF, sc)
        return c

    lax.fori_loop(0, k_top, pick, 0)
    for g in range(N_KV):
        picked = (pen_s[g, :, 0:Q_BLOCK] == NEG_INF) & (forced | valid)
        pen = jnp.where(picked, 0.0, NEG_INF)
        pen_s[g] = jnp.concatenate([pen] * Q_PER_KV, axis=1)

    n_tiles = (q0 + Q_BLOCK - 1) // KEY_TILE + 1
    n_far = jnp.maximum(q0 - MAX_DISTANCE, 0) // KEY_TILE
    nbt = KEY_TILE // SEL_BLOCK

    def qk(g, tt):
        k0 = pl.multiple_of(tt * KEY_TILE, KEY_TILE)
        return jnp.dot(selk_ref[g // 2, pl.ds(k0, KEY_TILE), :], wq_s[g], preferred_element_type=F32)

    def tile(tt, state, near):
        k0 = pl.multiple_of(tt * KEY_TILE, KEY_TILE)
        if near:
            r0 = pl.multiple_of(jnp.maximum(k0 - q0 + MAX_DISTANCE + NEAR_PAD, 0), Q_BLOCK)
            causal = k0 + lax.broadcasted_iota(jnp.int32, (KEY_TILE, 1), 0) <= t_lane
        out = []
        s_next = s0_s[...]
        for g in range(N_KV):
            s = s_next
            s_next = qk(g + 1, tt) if g + 1 < N_KV else qk(0, jnp.minimum(tt + 1, n_tiles - 1))
            m_old, acc = state[g]
            pen_t = pen_s[g, pl.ds(pl.multiple_of(tt * nbt, nbt), nbt), :]
            if near:
                bias = jnp.concatenate(
                    [nb_s[g * Q_PER_KV + r, pl.ds(r0, KEY_TILE), :] for r in range(Q_PER_KV)], axis=1)
                s = jnp.where(causal, s + bias, NEG_INF)
            else:
                s = s.astype(BF16)
                pen_t = pen_t.astype(BF16)
            s = (s.reshape(nbt, SEL_BLOCK, LANES_G) + pen_t[:, None, :]).reshape(KEY_TILE, LANES_G)
            m_new = jnp.maximum(m_old, jnp.max(s, axis=0, keepdims=True).astype(F32))
            alpha = jnp.exp2(m_old - m_new)
            p = jnp.exp2(s - m_new.astype(s.dtype)).astype(BF16)
            pv = jnp.dot(selvt_ref[g, :, pl.ds(k0, KEY_TILE)], p, preferred_element_type=F32)
            out.append((m_new, alpha * acc + pv))
        s0_s[...] = s_next
        return tuple(out)

    s0_s[...] = qk(0, 0)
    init = tuple((jnp.full((1, LANES_G), M_FLOOR, F32), jnp.zeros((V_ROWS, LANES_G), F32)) for _ in range(N_KV))
    far_done = lax.fori_loop(0, n_far, lambda tt, c: tile(tt, c, False), init)
    sel_done = lax.fori_loop(n_far, n_tiles, lambda tt, c: tile(tt, c, True), far_done)
    for g in range(N_KV):
        acc = sel_done[g][1]
        obr_s[1, g] = acc[0:HEAD_DIM] / acc[HEAD_DIM:HEAD_DIM + 1]

    for g in range(N_KV):
        for r in range(Q_PER_KV):
            h = g * Q_PER_KV + r
            sl = slice(r * Q_BLOCK, (r + 1) * Q_BLOCK)
            merged = (gt_ref[h:h + 1, :] * obr_s[0, g, :, sl] + gt_ref[N_HEADS + h:N_HEADS + h + 1, :] * obr_s[1, g, :, sl]
                      + gt_ref[2 * N_HEADS + h:2 * N_HEADS + h + 1, :] * obr_s[2, g, :, sl])
            opt_s[h * HEAD_DIM:(h + 1) * HEAD_DIM, :] = merged
    op = opt_s[...].T
    y = jnp.dot((op * _silu(u_ref[...])).astype(BF16), wout_ref[...], preferred_element_type=F32)
    y_ref[...] = _rms(x1_ref[...] + y, gfin_ref[...])


def _attn_prompt(rel_bias, qt, kck, kcvt, selk, selvt, wink, winvt, gt, u, x1, w_out, gfin):
    s = x1.shape[0]
    nq = s // Q_BLOCK
    nc = s // CMP_BLOCK
    nblk = s // SEL_BLOCK
    assert s % (2 * CMP_BLOCK * 128) == 0 and nblk >= TOP_N
    n_win = WINDOW // Q_BLOCK + 1
    full = lambda shp, **kw: pl.BlockSpec(shp, lambda i: (0,) * len(shp), **kw)
    once = dict(pipeline_mode=pl.Buffered(1))
    win_k_specs = [pl.BlockSpec((2, Q_BLOCK, 128), functools.partial(lambda i, k: (0, jnp.maximum(i - (n_win - 1) + k, 0), 0), k=k))
                   for k in range(n_win)]
    win_v_specs = [pl.BlockSpec((N_KV, HEAD_DIM, Q_BLOCK), functools.partial(lambda i, k: (0, 0, jnp.maximum(i - (n_win - 1) + k, 0)), k=k))
                   for k in range(n_win)]
    return pl.pallas_call(
        functools.partial(_attn_prompt_kernel, s_len=s),
        grid=(nq,),
        in_specs=[pl.BlockSpec(memory_space=pltpu.SMEM), full((N_HEADS, 128)),
                  pl.BlockSpec((N_KV, 256, Q_BLOCK), lambda i: (0, 0, i)),
                  full((2, nc, 128), **once), full((N_KV, HEAD_DIM, nc), **once),
                  full((2, s, 128), **once), full((N_KV, V_ROWS, s), **once),
                  *win_k_specs, *win_v_specs,
                  pl.BlockSpec((GATE_PAD, Q_BLOCK), lambda i: (0, i)),
                  pl.BlockSpec((Q_BLOCK, 1024), lambda i: (i, 0)),
                  pl.BlockSpec((Q_BLOCK, D_MODEL), lambda i: (i, 0)),
                  full((1024, D_MODEL), **once), full((1, D_MODEL))],
        out_specs=pl.BlockSpec((Q_BLOCK, D_MODEL), lambda i: (i, 0)),
        out_shape=jax.ShapeDtypeStruct((s, D_MODEL), F32),
        scratch_shapes=[pltpu.VMEM((N_HEADS, NEAR_ROWS, Q_BLOCK), F32),
                        pltpu.VMEM((n_win * Q_BLOCK, Q_BLOCK), F32),
                        pltpu.VMEM((N_HEADS, 8, 128), F32),
                        pltpu.VMEM((N_KV, 128, LANES_G), BF16),
                        pltpu.VMEM((N_KV, nblk, LANES_G), F32),
                        pltpu.VMEM((nc, LANES_G), F32),
                        pltpu.VMEM((KEY_TILE, LANES_G), F32),
                        pltpu.VMEM((3, N_KV, HEAD_DIM, LANES_G), F32),
                        pltpu.VMEM((N_HEADS * HEAD_DIM, Q_BLOCK), F32)],
        compiler_params=_cparams("arbitrary"),
        name="nsa_attn_prompt",
    )(rel_bias, jnp.pad(rel_bias.T, ((0, 0), (0, 128 - N_BUCKETS))), qt, kck, kcvt, selk, selvt,
      *([wink] * n_win), *([winvt] * n_win), gt, u, x1, w_out, gfin)


PAGES_PER_STEP = 16
CHUNK_PAGES = 32
STEPS_PER_CHUNK = CHUNK_PAGES // PAGES_PER_STEP
NEAR_PAGES = MAX_DISTANCE // PAGE_SIZE
NEW_PAD = 128


def _proj_sample_kernel(x_ref, g_ref, wn_ref, kvcs_ref, kvw_ref, q_ref, u_ref, gate_ref):
    xn = _rms(x_ref[...], g_ref[...]).astype(BF16)
    z = jnp.dot(xn, wn_ref[...], preferred_element_type=F32)
    kvcs_ref[...] = z[:, 0:1024]
    kvw_ref[...] = z[:, 1024:1536]
    q_ref[...] = (z[:, 1536:2560] * (HEAD_DIM ** -0.5)).astype(BF16)
    u_ref[...] = z[:, 2560:3584]
    gate_ref[...] = _sigmoid(z[:, 3584:3584 + GATE_PAD])


def _proj_sample(x1, g, w_nat):
    rows = x1.shape[0]
    sds = jax.ShapeDtypeStruct
    return pl.pallas_call(
        _proj_sample_kernel,
        out_shape=[sds((rows, 1024), F32), sds((rows, 512), F32), sds((rows, 1024), BF16),
                   sds((rows, 1024), F32), sds((rows, GATE_PAD), F32)],
        compiler_params=pltpu.CompilerParams(vmem_limit_bytes=VMEM_LIMIT),
        name="nsa_proj_sample",
    )(x1, g, w_nat)


def _softmax_lanes_online(s, m_old, l_old):
    m_new = jnp.maximum(m_old, jnp.max(s, axis=1, keepdims=True))
    alpha = jnp.exp(m_old - m_new)
    p = jnp.exp(s - m_new)
    return m_new, alpha, alpha * l_old + jnp.sum(p, axis=1, keepdims=True), p


def _attn_sample_kernel(pt_ref, *refs, past, ds, wb):
    pages = refs[:PAGES_PER_STEP]
    (q_ref, new_ref, wint_ref, gate_ref, u_ref, x1_ref, perm_ref, wk_ref, wv_ref, tb_ref, wout_ref, gfin_ref, y_ref,
     xl_s, kckt_s, kcv_s, skt_s, svt_s, bc_s, bs_s, bnew_s, bw_s, obr_s) = refs[PAGES_PER_STEP:]
    b = pl.program_id(0)
    pg = pl.program_id(1)
    npg = pl.num_programs(1)
    rows = Q_PER_KV * N_KV * ds
    nchunk = past // (CHUNK_PAGES * PAGE_SIZE)
    npages = past // PAGE_SIZE
    ncl = past // CMP_BLOCK
    nblk = past // SEL_BLOCK
    k_top = min(TOP_N - 1, nblk)
    nt = (((1,), (1,)), ((), ()))

    row_i = lax.broadcasted_iota(jnp.int32, (rows, 1), 0)
    tt = row_i % ds

    @pl.when((b == 0) & (pg == 0))
    def _():
        tbs = tb_ref[...] - tb_ref[:, N_BUCKETS - 1:N_BUCKETS]
        col = lambda k: tbs[:, k:k + 1]
        lane = lax.broadcasted_iota(jnp.int32, (1, ncl), 1)
        slot = lane % 8
        cblk = 8 * (lane // 8) + jnp.where(slot < 4, 2 * slot, 2 * (slot - 4) + 1)
        bc_s[...] = _bias_from_dist(past + tt - (CMP_BLOCK * cblk + CMP_BLOCK - 1), col)
        lane_p = lax.broadcasted_iota(jnp.int32, (1, PAGE_SIZE), 1)
        for k in range(NEAR_PAGES):
            bs_s[k] = _bias_from_dist(tt + MAX_DISTANCE - PAGE_SIZE * k - lane_p, col)
        bnew_s[...] = _bias_from_dist(tt - lax.broadcasted_iota(jnp.int32, (1, NEW_PAD), 1), col)
        bw_s[...] = _bias_from_dist(tt + wb - lax.broadcasted_iota(jnp.int32, (1, wb), 1), col)

    for pair in range(PAGES_PER_STEP // 2):
        pa, pb = pages[2 * pair], pages[2 * pair + 1]
        slab = pl.ds(pl.multiple_of(((pg % STEPS_PER_CHUNK) * (PAGES_PER_STEP // 2) + pair) * 8, 8), 8)
        for kind in range(2):
            xt2 = jnp.concatenate([pa[0, kind * 256:(kind + 1) * 256, :], pb[0, kind * 256:(kind + 1) * 256, :]],
                                  axis=1).astype(BF16)
            x_perm = lax.dot_general(perm_ref[...], xt2, nt, preferred_element_type=F32)
            xl_s[:, slab, kind * 256:(kind + 1) * 256] = x_perm.reshape(CMP_BLOCK, 8, 256)
    for k in range(PAGES_PER_STEP):
        skt_s[pg * PAGES_PER_STEP + k] = pages[k][0, 512:768, :].astype(BF16)
        svt_s[pg * PAGES_PER_STEP + k] = pages[k][0, 768:1024, :].astype(BF16)

    @pl.when(pg % STEPS_PER_CHUNK == STEPS_PER_CHUNK - 1)
    def _():
        nbc = CHUNK_PAGES * PAGE_SIZE // CMP_BLOCK

        def body(l, acc):
            ak, av = acc
            xk = xl_s[l, :, 0:256].astype(BF16)
            xv = xl_s[l, :, 256:512].astype(BF16)
            ak = ak + jnp.dot(xk, wk_ref[l], preferred_element_type=F32)
            av = av + jnp.dot(xv, wv_ref[l], preferred_element_type=F32)
            return ak, av

        ak, av = lax.fori_loop(0, CMP_BLOCK, body, (jnp.zeros((nbc, 256), F32), jnp.zeros((nbc, 256), F32)),
                               unroll=4)
        ch = pg // STEPS_PER_CHUNK
        kckt_s[ch] = ak.T.astype(BF16)
        kcv_s[ch] = av.astype(BF16)

    @pl.when(pg == npg - 1)
    def _():
        lane_g = lax.broadcasted_iota(jnp.int32, (1, 256), 1) // HEAD_DIM
        row_g = (row_i // ds) % N_KV
        wq = jnp.where(row_g == lane_g, q_ref[0], jnp.zeros((), BF16))
        t_row = past + tt

        s_c = jnp.concatenate([jnp.dot(wq, kckt_s[ch], preferred_element_type=F32) for ch in range(nchunk)],
                              axis=1) + bc_s[...]
        m = jnp.max(s_c, axis=1, keepdims=True)
        e = jnp.exp(s_c - m)
        p_c = e / jnp.sum(e, axis=1, keepdims=True)
        o_c = jnp.zeros((rows, 256), F32)
        for ch in range(nchunk):
            o_c = o_c + jnp.dot(p_c[:, ch * 128:(ch + 1) * 128].astype(BF16), kcv_s[ch], preferred_element_type=F32)
        obr_s[0] = o_c

        gt_rows = N_KV * ds
        ps = p_c[0:gt_rows]
        for r in range(1, Q_PER_KV):
            ps = ps + p_c[r * gt_rows:(r + 1) * gt_rows]
        imp = ps + pltpu.roll(ps, ncl - 4, axis=1)
        lane = lax.broadcasted_iota(jnp.int32, (gt_rows, ncl), 1)
        slot = lane % 8
        jblk = 4 * (lane // 8) + slot
        t_gt = past + lax.broadcasted_iota(jnp.int32, (gt_rows, ncl), 0) % ds
        forced = (jblk == 0) | (jblk == t_gt // SEL_BLOCK)
        valid = jblk * SEL_BLOCK <= t_gt
        scores = jnp.where(slot < 4, jnp.where(forced, jnp.inf, jnp.where(valid, imp, NEG_INF)), NEG_INF)
        jf = jnp.where(slot < 4, jblk, 2 * nblk).astype(F32)
        picks = []
        for _ in range(k_top):
            mx = jnp.max(scores, axis=1, keepdims=True)
            idx = jnp.min(jnp.where(scores == mx, jf, float(2 * nblk)), axis=1, keepdims=True)
            scores = jnp.where(jf == idx, NEG_INF, scores)
            picks.append(idx)

        gk = NEAR_PAGES
        blk_lane = (lax.broadcasted_iota(jnp.int32, (1, gk * PAGE_SIZE), 1) // SEL_BLOCK).astype(F32)

        def pages_step(pgi, carry, bias):
            m_old, l_old, acc = carry
            s = jnp.concatenate([jnp.dot(wq, skt_s[pgi * gk + k], preferred_element_type=F32) for k in range(gk)],
                                axis=1)
            if bias is not None:
                s = s + bias
            jk = blk_lane + jnp.asarray(pgi * (gk * PAGE_SIZE // SEL_BLOCK), F32)
            hit = jk == picks[0]
            for pk in picks[1:]:
                hit = hit | (jk == pk)
            pen = jnp.where(hit, 0.0, NEG_INF)
            s = (s.reshape(Q_PER_KV, gt_rows, gk * PAGE_SIZE) + pen[None]).reshape(rows, gk * PAGE_SIZE)
            m_new, alpha, l_new, pr = _softmax_lanes_online(s, m_old, l_old)
            prb = pr.astype(BF16)
            pv = alpha * acc
            for k in range(gk):
                pv = pv + lax.dot_general(prb[:, k * PAGE_SIZE:(k + 1) * PAGE_SIZE], svt_s[pgi * gk + k], nt,
                                          preferred_element_type=F32)
            return m_new, l_new, pv

        carry = (jnp.full((rows, 1), M_FLOOR, F32), jnp.zeros((rows, 1), F32), jnp.zeros((rows, 256), F32))
        carry = lax.fori_loop(0, npages // gk - 1, lambda j, c: pages_step(j, c, None), carry)
        bias_near = jnp.concatenate([bs_s[k] for k in range(NEAR_PAGES)], axis=1)
        m_old, l_old, acc = pages_step(npages // gk - 1, carry, bias_near)
        lane_n = lax.broadcasted_iota(jnp.int32, (1, NEW_PAD), 1)
        ok_new = (lane_n < ds) & (lane_n <= tt)
        s = lax.dot_general(wq, new_ref[0, 0], nt, preferred_element_type=F32) + bnew_s[...]
        m_new, alpha, l_new, pr = _softmax_lanes_online(jnp.where(ok_new, s, NEG_INF), m_old, l_old)
        acc = alpha * acc + jnp.dot(pr.astype(BF16), new_ref[0, 1], preferred_element_type=F32)
        obr_s[1] = acc / l_new

        kwt = wint_ref[0, 0:256, :].astype(BF16)
        vwt = wint_ref[0, 256:512, :].astype(BF16)
        dist = tt + wb - lax.broadcasted_iota(jnp.int32, (1, wb), 1)
        s_w = jnp.where((dist >= 0) & (dist < WINDOW), jnp.dot(wq, kwt, preferred_element_type=F32) + bw_s[...], NEG_INF)
        s_n = jnp.where(ok_new, lax.dot_general(wq, new_ref[0, 2], nt, preferred_element_type=F32) + bnew_s[...], NEG_INF)
        m = jnp.maximum(jnp.max(s_w, axis=1, keepdims=True), jnp.max(s_n, axis=1, keepdims=True))
        e_w = jnp.exp(s_w - m)
        e_n = jnp.exp(s_n - m)
        den = jnp.sum(e_w, axis=1, keepdims=True) + jnp.sum(e_n, axis=1, keepdims=True)
        o_w = (lax.dot_general(e_w.astype(BF16), vwt, nt, preferred_element_type=F32)
               + jnp.dot(e_n.astype(BF16), new_ref[0, 3], preferred_element_type=F32))
        obr_s[2] = o_w / den

        merged = []
        for r in range(Q_PER_KV):
            tot = None
            for br in range(3):
                o_r = jnp.zeros((ds, 256), F32)
                g_r = jnp.zeros((ds, 256), F32)
                for g in range(N_KV):
                    base = r * gt_rows + g * ds
                    o_r = jnp.where(lane_g == g, obr_s[br, base:base + ds, :], o_r)
                    c = br * N_HEADS + g * Q_PER_KV + r
                    g_r = jnp.where(lane_g == g, gate_ref[0, :, c:c + 1], g_r)
                tot = g_r * o_r if tot is None else tot + g_r * o_r
            merged.append(tot)
        op = jnp.concatenate(merged, axis=1)
        y = jnp.dot((op * _silu(u_ref[0])).astype(BF16), wout_ref[...], preferred_element_type=F32)
        y_ref[0] = _rms(x1_ref[0] + y, gfin_ref[...])


def _attn_sample(page_table, pool_t, q64, newkv, win_t, gates, u_r, x1, perm, wkt_bd, wv_bd, tb_rows, w_out_r, gfin,
                 *, past, ds, wb):
    nb = page_table.shape[0]
    npages = past // PAGE_SIZE
    assert past % (CHUNK_PAGES * PAGE_SIZE) == 0 and npages >= NEAR_PAGES and past // SEL_BLOCK >= TOP_N
    npg = npages // PAGES_PER_STEP
    nchunk = npages // CHUNK_PAGES
    rows = Q_PER_KV * N_KV * ds
    ncl = past // CMP_BLOCK
    once = dict(pipeline_mode=pl.Buffered(1))
    full = lambda shp, **kw: pl.BlockSpec(shp, lambda b, p, pt: (0,) * len(shp), **kw)
    per_b = lambda shp: pl.BlockSpec((1,) + shp, lambda b, p, pt: (b,) + (0,) * len(shp))
    page_specs = [pl.BlockSpec((1, 1024, PAGE_SIZE),
                               functools.partial(lambda b, p, pt, k: (pt[b, p * PAGES_PER_STEP + k], 0, 0), k=k))
                  for k in range(PAGES_PER_STEP)]
    grid_spec = pltpu.PrefetchScalarGridSpec(
        num_scalar_prefetch=1, grid=(nb, npg),
        in_specs=[*page_specs, per_b((rows, 256)), per_b((4, NEW_PAD, 256)), per_b((512, wb)),
                  per_b((ds, GATE_PAD)), per_b((ds, 1024)), per_b((ds, D_MODEL)),
                  full((256, 256), **once), full((CMP_BLOCK, 256, 256), **once), full((CMP_BLOCK, 256, 256), **once),
                  full((rows, N_BUCKETS)), full((1024, D_MODEL), **once), full((1, D_MODEL))],
        out_specs=pl.BlockSpec((1, ds, D_MODEL), lambda b, p, pt: (b, 0, 0)),
        scratch_shapes=[pltpu.VMEM((CMP_BLOCK, CHUNK_PAGES * PAGE_SIZE // CMP_BLOCK, 512), F32),
                        pltpu.VMEM((nchunk, 256, 128), BF16), pltpu.VMEM((nchunk, 128, 256), BF16),
                        pltpu.VMEM((npages, 256, PAGE_SIZE), BF16), pltpu.VMEM((npages, 256, PAGE_SIZE), BF16),
                        pltpu.VMEM((rows, ncl), F32), pltpu.VMEM((NEAR_PAGES, rows, PAGE_SIZE), F32),
                        pltpu.VMEM((rows, NEW_PAD), F32), pltpu.VMEM((rows, wb), F32),
                        pltpu.VMEM((3, rows, 256), F32)])
    return pl.pallas_call(
        functools.partial(_attn_sample_kernel, past=past, ds=ds, wb=wb),
        grid_spec=grid_spec,
        out_shape=jax.ShapeDtypeStruct((nb, ds, D_MODEL), F32),
        compiler_params=_cparams("arbitrary", "arbitrary"),
        name="nsa_attn_sample",
    )(page_table, *([pool_t] * PAGES_PER_STEP), q64, newkv, win_t, gates, u_r, x1, perm, wkt_bd, wv_bd, tb_rows,
      w_out_r, gfin)


def _pair_permutation():
    m = jnp.arange(256)
    l, slot = m // 8, m % 8
    cb = jnp.where(slot < 4, 2 * slot, 2 * (slot - 4) + 1)
    tok = CMP_BLOCK * cb + l
    return (tok[:, None] == jnp.arange(256)[None, :]).astype(BF16)


def _block_diag4(w):
    eye = jnp.eye(N_KV, dtype=w.dtype)
    out = eye[:, None, :, None] * w[..., None, :, None, :]
    return out.reshape(*w.shape[:-2], N_KV * HEAD_DIM, N_KV * HEAD_DIM)


def kernel(x_prompt, x_sample, state_lru_h, state_conv, cache_nsa_kv, cache_win_kv, page_table, norm_g,
           final_norm_g, w_in_lru, conv_w, conv_b, w_gate_a, b_gate_a, w_gate_x, b_gate_x, lru_lambda,
           w_out_lru, w_in_nsa, w_cmp, w_out_nsa, rel_bias):
    bp, s_len, _ = x_prompt.shape
    nb, ds, _ = x_sample.shape
    assert bp == 1
    row = lambda v: v.reshape(1, -1)

    lru_w = (row(norm_g[0]), w_in_lru[0].astype(BF16), conv_w[0], row(conv_b[0]), w_gate_a[0].astype(BF16),
             row(b_gate_a[0]), w_gate_x[0].astype(BF16), row(b_gate_x[0]), row(lru_lambda[0]),
             w_out_lru[0].astype(BF16))
    x1p, hp, tailp = _lru_prompt(x_prompt[0], *lru_w)
    xs_tm = x_sample.transpose(1, 0, 2).reshape(ds * nb, D_MODEL)
    buf_tm = state_conv[0].transpose(1, 0, 2).reshape((CONV_W - 1) * nb, D_MODEL)
    x1s_tm, hs_last, tails = _lru_sample(xs_tm, buf_tm, state_lru_h[0], *lru_w, nb=nb, ds=ds)

    prompt_lru_h = hp.reshape(1, 1, D_MODEL)
    prompt_conv = tailp[8 - (CONV_W - 1):].reshape(1, 1, CONV_W - 1, D_MODEL)
    sample_lru_h = hs_last.reshape(1, nb, D_MODEL)
    sample_conv = tails.reshape(CONV_W - 1, nb, D_MODEL).transpose(1, 0, 2)[None]

    w = w_in_nsa[0]
    wq, wkv = w[:, :1024], w[:, 1024:1024 + KV_WIDTH]
    wg, wu = w[:, 1024 + KV_WIDTH:1024 + KV_WIDTH + 3 * N_HEADS], w[:, 1024 + KV_WIDTH + 3 * N_HEADS:]
    wg_pad = jnp.pad(wg, ((0, 0), (0, GATE_PAD - 3 * N_HEADS)))
    w_nat_p = jnp.concatenate([wkv[:, 0:768], wkv[:, 1024:1280], wu], axis=1).astype(BF16)
    w_t_p = jnp.concatenate([wq.T, wkv.T, wg_pad.T], axis=0).astype(BF16)
    wk_bd = _block_diag4(w_cmp[0, 0]).astype(BF16)
    wvt_bd = jnp.swapaxes(_block_diag4(w_cmp[0, 1]), -1, -2).astype(BF16)
    w_out = w_out_nsa[0].astype(BF16)
    g1 = row(norm_g[1])
    gfin = row(final_norm_g)

    cmp_kv, selk, wink, u_p, qt, kvcst, kvwt, selvt, winvt, gt = _proj_prompt(x1p, g1, w_nat_p, w_t_p)
    kck, kcvt = _compress_prompt(cmp_kv, wk_bd, wvt_bd)
    y_prompt = _attn_prompt(rel_bias, qt, kck, kcvt, selk, selvt, wink, winvt, gt, u_p, x1p, w_out, gfin)
    y_prompt = y_prompt[None]
    prompt_kv_rows = kvcst.reshape(4, N_KV, HEAD_DIM, s_len).transpose(3, 0, 1, 2)[None, None]
    wbp = min(WINDOW, s_len)
    prompt_win_kv = kvwt[:, s_len - wbp:].reshape(2, N_KV, HEAD_DIM, wbp).transpose(3, 0, 1, 2)[None, None]

    n_phys, n_layers = cache_nsa_kv.shape[:2]
    past = page_table.shape[1] * PAGE_SIZE
    wb = cache_win_kv.shape[2]
    to_rgd = lambda m: m.reshape(m.shape[0], N_KV, Q_PER_KV, HEAD_DIM).transpose(0, 2, 1, 3).reshape(m.shape[0], 1024)
    w_nat_s = jnp.concatenate([wkv, to_rgd(wq), to_rgd(wu), wg_pad], axis=1).astype(BF16)
    w_out_r = to_rgd(w_out_nsa[0].T).T.astype(BF16)
    x1s = x1s_tm.reshape(ds, nb, D_MODEL).transpose(1, 0, 2).reshape(nb * ds, D_MODEL)
    kvcs_s, kvw_s, q_s, u_s, gate_s = _proj_sample(x1s, g1, w_nat_s)
    q64 = q_s.reshape(nb, ds, Q_PER_KV, 1, 256).transpose(0, 2, 3, 1, 4)
    q64 = jnp.broadcast_to(q64, (nb, Q_PER_KV, N_KV, ds, 256)).reshape(nb, Q_PER_KV * N_KV * ds, 256)
    new4 = jnp.stack([kvcs_s[:, 512:768], kvcs_s[:, 768:1024], kvw_s[:, 0:256], kvw_s[:, 256:512]], axis=0)
    new4 = new4.reshape(4, nb, ds, 256).transpose(1, 0, 2, 3).astype(BF16)
    newkv = jnp.pad(new4, ((0, 0), (0, 0), (0, NEW_PAD - ds), (0, 0)))
    pool_t = cache_nsa_kv.transpose(0, 1, 3, 4, 5, 2).reshape(n_phys * n_layers, 1024, PAGE_SIZE)
    win_t = cache_win_kv[0].transpose(0, 2, 3, 4, 1).reshape(nb, 512, wb)
    head_of_row = (jnp.arange(Q_PER_KV * N_KV * ds) // ds % N_KV) * Q_PER_KV + jnp.arange(Q_PER_KV * N_KV * ds) // (N_KV * ds)
    tb_rows = rel_bias.T[head_of_row]
    y_sample = _attn_sample(page_table * n_layers, pool_t, q64, newkv, win_t, gate_s.reshape(nb, ds, GATE_PAD),
                            u_s.reshape(nb, ds, 1024), x1s.reshape(nb, ds, D_MODEL), _pair_permutation(),
                            wk_bd, jnp.swapaxes(wvt_bd, -1, -2), tb_rows, w_out_r, gfin,
                            past=past, ds=ds, wb=wb)
    sample_kv_rows = kvcs_s.reshape(nb, 1, ds, 4, N_KV, HEAD_DIM)
    new_t = kvw_s.reshape(nb, ds, 512).transpose(0, 2, 1)
    win_all = jnp.concatenate([win_t, new_t], axis=2)[:, :, ds:]
    sample_win_kv = win_all.reshape(nb, 2, N_KV, HEAD_DIM, wb).transpose(0, 4, 1, 2, 3)[None]

    return (y_prompt, y_sample, prompt_lru_h, prompt_conv, prompt_kv_rows, prompt_win_kv,
            sample_lru_h, sample_conv, sample_kv_rows, sample_win_kv)
```

```python
import functools
import math

import jax
import jax.numpy as jnp
from jax import lax
from jax.experimental import pallas as pl
from jax.experimental.pallas import tpu as pltpu

F32 = jnp.float32
BF16 = jnp.bfloat16

D_MODEL = 1024
HEAD_DIM = 64
N_HEADS = 16
N_KV = 4
Q_PER_KV = 4
CMP_BLOCK = 32
SEL_BLOCK = 64
TOP_N = 16
WINDOW = 512
PAGE_SIZE = 128
N_BUCKETS = 32
MAX_DISTANCE = 1024
N_LRU_BLOCKS = 8
LRU_BLOCK = 128
CONV_W = 4
LRU_C = 8.0
RMS_EPS = 1e-6

V7X_VMEM_BYTES = 64 * 1024 * 1024
VMEM_LIMIT = V7X_VMEM_BYTES - 8 * 1024 * 1024


def _bucket_lower_bounds():
    max_exact = N_BUCKETS // 2
    lows = list(range(1, max_exact + 1))
    for k in range(1, N_BUCKETS - max_exact):
        n = max_exact
        while n ** 8 < 2 ** (32 + 3 * k):
            n += 1
        lows.append(n)
    return lows


_BUCKET_LOWS = _bucket_lower_bounds()


def _bias_from_dist(dist, table_col):
    val = jnp.full(dist.shape, table_col(0), F32)
    for b, lo in enumerate(_BUCKET_LOWS, start=1):
        val = jnp.where(dist >= lo, table_col(b), val)
    return val


def _rms(x, g):
    ms = jnp.mean(x * x, axis=-1, keepdims=True)
    return x * lax.rsqrt(ms + RMS_EPS) * g


def _sigmoid(x):
    return 1.0 / (1.0 + jnp.exp(-x))


def _silu(x):
    return x * _sigmoid(x)


def _softplus(x):
    return jnp.maximum(x, 0.0) + jnp.log1p(jnp.exp(-jnp.abs(x)))


def _expm1(x):
    u = jnp.exp(x)
    um1 = u - 1.0
    y = um1 * x / jnp.log(u)
    y = jnp.where(u == 1.0, x, y)
    return jnp.where(um1 == -1.0, -1.0, y)


def _cparams(*sem):
    return pltpu.CompilerParams(dimension_semantics=sem, vmem_limit_bytes=VMEM_LIMIT)


def _lru_gate_terms(conv, wa_ref, ba, wx_ref, bx, lam):
    cb = conv.astype(BF16)
    r_parts, i_parts = [], []
    for n in range(N_LRU_BLOCKS):
        c = cb[:, n * LRU_BLOCK:(n + 1) * LRU_BLOCK]
        r_parts.append(jnp.dot(c, wa_ref[n], preferred_element_type=F32))
        i_parts.append(jnp.dot(c, wx_ref[n], preferred_element_type=F32))
    r = _sigmoid(jnp.concatenate(r_parts, axis=1) + ba)
    i = _sigmoid(jnp.concatenate(i_parts, axis=1) + bx)
    log_a = -LRU_C * r * _softplus(-lam)
    a = jnp.exp(log_a)
    u = jnp.sqrt(-_expm1(2.0 * log_a)) * i * conv
    return a, u


def _lru_prompt_kernel(x_ref, g_ref, win_ref, cw_ref, cb_ref, wa_ref, ba_ref, wx_ref, bx_ref,
                       lam_ref, wout_ref, x1_ref, hlast_ref, tail_ref,
                       xb_ext, a_s, u_s, h_c, *, tr):
    i = pl.program_id(0)

    @pl.when(i == 0)
    def _():
        xb_ext[0:8, :] = jnp.zeros((8, D_MODEL), F32)
        h_c[...] = jnp.zeros_like(h_c)

    x = x_ref[...]
    xn = _rms(x, g_ref[...])
    z = jnp.dot(xn.astype(BF16), win_ref[...], preferred_element_type=F32)
    xb = z[:, :D_MODEL]
    gate = z[:, D_MODEL:]
    xb_ext[8:8 + tr, :] = xb
    conv = cb_ref[...]
    for j in range(CONV_W):
        conv = conv + cw_ref[j:j + 1, :] * xb_ext[8 - (CONV_W - 1) + j:8 - (CONV_W - 1) + j + tr, :]
    a, u = _lru_gate_terms(conv, wa_ref, ba_ref[...], wx_ref, bx_ref[...], lam_ref[...])
    a_s[...] = a
    u_s[...] = u

    row = lax.broadcasted_iota(jnp.int32, (8, D_MODEL), 0)

    def body(k, h):
        base = pl.multiple_of(k * 8, 8)
        aa = a_s[pl.ds(base, 8), :]
        uu = u_s[pl.ds(base, 8), :]
        for s in (1, 2, 4):
            keep = row >= s
            us = jnp.where(keep, pltpu.roll(uu, s, axis=0), 0.0)
            as_ = jnp.where(keep, pltpu.roll(aa, s, axis=0), 1.0)
            uu = uu + aa * us
            aa = aa * as_
        hb = aa * h + uu
        u_s[pl.ds(base, 8), :] = hb
        return hb[7:8, :]

    h_last = lax.fori_loop(0, tr // 8, body, h_c[...])
    h_c[...] = h_last
    hlast_ref[...] = h_last
    tail_ref[...] = xb[tr - 8:, :]
    xb_ext[0:8, :] = xb[tr - 8:, :]
    y = u_s[...] * _silu(gate)
    out = jnp.dot(y.astype(BF16), wout_ref[...], preferred_element_type=F32)
    x1_ref[...] = x + out


def _lru_prompt(x, g, w_in, cw, cb, wa, ba, wx, bx, lam, w_out, tr=512):
    s = x.shape[0]
    tr = min(tr, s)
    assert s % tr == 0 and tr % 8 == 0
    full = lambda shp: pl.BlockSpec(shp, lambda i: (0,) * len(shp))
    return pl.pallas_call(
        functools.partial(_lru_prompt_kernel, tr=tr),
        grid=(s // tr,),
        in_specs=[pl.BlockSpec((tr, D_MODEL), lambda i: (i, 0)), full((1, D_MODEL)),
                  full((D_MODEL, 2 * D_MODEL)), full((CONV_W, D_MODEL)), full((1, D_MODEL)),
                  full((N_LRU_BLOCKS, LRU_BLOCK, LRU_BLOCK)), full((1, D_MODEL)),
                  full((N_LRU_BLOCKS, LRU_BLOCK, LRU_BLOCK)), full((1, D_MODEL)),
                  full((1, D_MODEL)), full((D_MODEL, D_MODEL))],
        out_specs=[pl.BlockSpec((tr, D_MODEL), lambda i: (i, 0)), full((1, D_MODEL)), full((8, D_MODEL))],
        out_shape=[jax.ShapeDtypeStruct((s, D_MODEL), F32), jax.ShapeDtypeStruct((1, D_MODEL), F32),
                   jax.ShapeDtypeStruct((8, D_MODEL), F32)],
        scratch_shapes=[pltpu.VMEM((tr + 8, D_MODEL), F32), pltpu.VMEM((tr, D_MODEL), F32),
                        pltpu.VMEM((tr, D_MODEL), F32), pltpu.VMEM((1, D_MODEL), F32)],
        compiler_params=_cparams("arbitrary"),
        name="lru_prompt",
    )(x, g, w_in, cw, cb, wa, ba, wx, bx, lam, w_out)


def _lru_sample_kernel(x_ref, buf_ref, h0_ref, g_ref, win_ref, cw_ref, cb_ref, wa_ref, ba_ref, wx_ref,
                       bx_ref, lam_ref, wout_ref, x1_ref, hlast_ref, tail_ref, xp_s, *, nb, ds):
    x = x_ref[...]
    xn = _rms(x, g_ref[...])
    z = jnp.dot(xn.astype(BF16), win_ref[...], preferred_element_type=F32)
    xb = z[:, :D_MODEL]
    gate = z[:, D_MODEL:]
    nbuf = (CONV_W - 1) * nb
    xp_s[0:nbuf, :] = buf_ref[...]
    xp_s[nbuf:, :] = xb
    conv = cb_ref[...]
    for j in range(CONV_W):
        conv = conv + cw_ref[j:j + 1, :] * xp_s[j * nb:j * nb + ds * nb, :]
    a, u = _lru_gate_terms(conv, wa_ref, ba_ref[...], wx_ref, bx_ref[...], lam_ref[...])
    h = h0_ref[...]
    hs = []
    for t in range(ds):
        h = a[t * nb:(t + 1) * nb, :] * h + u[t * nb:(t + 1) * nb, :]
        hs.append(h)
    hlast_ref[...] = h
    tail_ref[...] = xp_s[ds * nb:, :]
    y = jnp.concatenate(hs, axis=0) * _silu(gate)
    out = jnp.dot(y.astype(BF16), wout_ref[...], preferred_element_type=F32)
    x1_ref[...] = x + out


def _lru_sample(x_tm, buf_tm, h0, g, w_in, cw, cb, wa, ba, wx, bx, lam, w_out, nb, ds):
    rows = nb * ds
    nbuf = (CONV_W - 1) * nb
    return pl.pallas_call(
        functools.partial(_lru_sample_kernel, nb=nb, ds=ds),
        out_shape=[jax.ShapeDtypeStruct((rows, D_MODEL), F32), jax.ShapeDtypeStruct((nb, D_MODEL), F32),
                   jax.ShapeDtypeStruct((nbuf, D_MODEL), F32)],
        scratch_shapes=[pltpu.VMEM((nbuf + rows, D_MODEL), F32)],
        compiler_params=pltpu.CompilerParams(vmem_limit_bytes=VMEM_LIMIT),
        name="lru_sample",
    )(x_tm, buf_tm, h0, g, w_in, cw, cb, wa, ba, wx, bx, lam, w_out)


Q_BLOCK = 128
LANES_G = Q_PER_KV * Q_BLOCK
KEY_TILE = 512
NEAR_PAD = KEY_TILE
NEAR_ROWS = NEAR_PAD + MAX_DISTANCE + KEY_TILE
NEG_INF = float("-inf")
LOG2E = math.log2(math.e)
M_FLOOR = -1e30

KV_WIDTH = 3 * 2 * N_KV * HEAD_DIM
GATE_PAD = 128
V_ROWS = HEAD_DIM + 16


def _proj_prompt_kernel(x_ref, g_ref, wn_ref, wt_ref, cmp_ref, selk_ref, wink_ref, u_ref,
                        qt_ref, kvcst_ref, kvwt_ref, selvt_ref, winvt_ref, gt_ref, *, tr):
    xn = _rms(x_ref[...], g_ref[...]).astype(BF16)
    z = jnp.dot(xn, wn_ref[...], preferred_element_type=F32)
    cmp_ref[...] = z[:, 0:512].astype(BF16)
    for p in range(2):
        selk_ref[p] = z[:, 512 + 128 * p:640 + 128 * p].astype(BF16)
        wink_ref[p] = z[:, 768 + 128 * p:896 + 128 * p].astype(BF16)
    u_ref[...] = z[:, 1024:2048]
    zt = lax.dot_general(wt_ref[...], xn, (((1,), (1,)), ((), ())), preferred_element_type=F32)
    qt_ref[...] = (zt[0:1024] * (HEAD_DIM ** -0.5 * LOG2E)).astype(BF16).reshape(N_KV, Q_PER_KV * HEAD_DIM, tr)
    kvt = zt[1024:1024 + KV_WIDTH]
    kvcst_ref[...] = kvt[0:1024]
    kvwt_ref[...] = kvt[1024:1536]
    selvt_ref[:, 0:HEAD_DIM, :] = kvt[768:1024].astype(BF16).reshape(N_KV, HEAD_DIM, tr)
    pad_row = lax.broadcasted_iota(jnp.int32, (N_KV, V_ROWS - HEAD_DIM, tr), 1)
    selvt_ref[:, HEAD_DIM:V_ROWS, :] = jnp.where(pad_row == 0, 1.0, 0.0).astype(BF16)
    winvt_ref[...] = kvt[1280:1536].astype(BF16).reshape(N_KV, HEAD_DIM, tr)
    gt_ref[...] = _sigmoid(zt[1024 + KV_WIDTH:1024 + KV_WIDTH + GATE_PAD])


def _proj_prompt(x1, g, w_nat, w_t, tr=512):
    s = x1.shape[0]
    tr = min(tr, s)
    nn, nt = w_nat.shape[1], w_t.shape[0]
    full = lambda shp: pl.BlockSpec(shp, lambda i: (0,) * len(shp))
    sds = jax.ShapeDtypeStruct
    return pl.pallas_call(
        functools.partial(_proj_prompt_kernel, tr=tr),
        grid=(s // tr,),
        in_specs=[pl.BlockSpec((tr, D_MODEL), lambda i: (i, 0)), full((1, D_MODEL)), full((D_MODEL, nn)),
                  full((nt, D_MODEL))],
        out_specs=[pl.BlockSpec((tr, 512), lambda i: (i, 0)),
                   pl.BlockSpec((2, tr, 128), lambda i: (0, i, 0)), pl.BlockSpec((2, tr, 128), lambda i: (0, i, 0)),
                   pl.BlockSpec((tr, 1024), lambda i: (i, 0)),
                   pl.BlockSpec((N_KV, 256, tr), lambda i: (0, 0, i)),
                   pl.BlockSpec((1024, tr), lambda i: (0, i)), pl.BlockSpec((512, tr), lambda i: (0, i)),
                   pl.BlockSpec((N_KV, V_ROWS, tr), lambda i: (0, 0, i)),
                   pl.BlockSpec((N_KV, HEAD_DIM, tr), lambda i: (0, 0, i)),
                   pl.BlockSpec((GATE_PAD, tr), lambda i: (0, i))],
        out_shape=[sds((s, 512), BF16), sds((2, s, 128), BF16), sds((2, s, 128), BF16),
                   sds((s, 1024), F32), sds((N_KV, 256, s), BF16), sds((1024, s), F32), sds((512, s), F32),
                   sds((N_KV, V_ROWS, s), BF16), sds((N_KV, HEAD_DIM, s), BF16), sds((GATE_PAD, s), F32)],
        compiler_params=_cparams("arbitrary"),
        name="nsa_proj_prompt",
    )(x1, g, w_nat, w_t)


def _compress_prompt_kernel(xe_ref, xo_ref, wk_ref, wvt_ref, kck_ref, kcvt_ref, acck, accv, *, tl, nh):
    step = pl.program_id(0)

    @pl.when(step == 0)
    def _():
        acck[...] = jnp.zeros_like(acck)
        accv[...] = jnp.zeros_like(accv)

    for half, x_ref in enumerate((xe_ref, xo_ref)):
        for l in range(tl):
            xk = x_ref[:, l * 512:l * 512 + 256]
            xv = x_ref[:, l * 512 + 256:l * 512 + 512]
            acck[half * nh:(half + 1) * nh, :] += jnp.dot(xk, wk_ref[l], preferred_element_type=F32)
            accv[:, half * nh:(half + 1) * nh] += lax.dot_general(
                wvt_ref[l], xv, (((1,), (1,)), ((), ())), preferred_element_type=F32)

    @pl.when(step == pl.num_programs(0) - 1)
    def _():
        for p in range(2):
            kck_ref[p] = acck[:, 128 * p:128 * (p + 1)].astype(BF16)
        kcvt_ref[...] = accv[...].astype(BF16).reshape(N_KV, HEAD_DIM, 2 * nh)


def _compress_prompt(cmp_kv, wk_bd, wvt_bd, tl=4):
    s = cmp_kv.shape[0]
    nh = s // (2 * CMP_BLOCK)
    x2 = cmp_kv.reshape(nh, 2 * CMP_BLOCK * 512)
    nsteps = CMP_BLOCK // tl
    return pl.pallas_call(
        functools.partial(_compress_prompt_kernel, tl=tl, nh=nh),
        grid=(nsteps,),
        in_specs=[pl.BlockSpec((nh, tl * 512), lambda i: (0, i)),
                  pl.BlockSpec((nh, tl * 512), lambda i: (0, nsteps + i)),
                  pl.BlockSpec((tl, 256, 256), lambda i: (i, 0, 0)),
                  pl.BlockSpec((tl, 256, 256), lambda i: (i, 0, 0))],
        out_specs=[pl.BlockSpec((2, 2 * nh, 128), lambda i: (0, 0, 0)),
                   pl.BlockSpec((N_KV, HEAD_DIM, 2 * nh), lambda i: (0, 0, 0))],
        out_shape=[jax.ShapeDtypeStruct((2, 2 * nh, 128), BF16),
                   jax.ShapeDtypeStruct((N_KV, HEAD_DIM, 2 * nh), BF16)],
        scratch_shapes=[pltpu.VMEM((2 * nh, 256), F32), pltpu.VMEM((256, 2 * nh), F32)],
        compiler_params=_cparams("arbitrary"),
        name="nsa_compress_prompt",
    )(x2, x2, wk_bd, wvt_bd)


def _softmax_rows(s):
    m = jnp.max(s, axis=0, keepdims=True)
    m = jnp.where(m == NEG_INF, 0.0, m)
    e = jnp.exp2(s - m)
    den = jnp.sum(e, axis=0, keepdims=True)
    return e / jnp.where(den > 0, den, 1.0)


def _attn_prompt_kernel(bias_ref, tblt_ref, qt_ref, kck_ref, kcvt_ref, selk_ref, selvt_ref,
                        wk0, wk1, wk2, wk3, wk4, wv0, wv1, wv2, wv3, wv4,
                        gt_ref, u_ref, x1_ref, wout_ref, gfin_ref, y_ref,
                        nb_s, wpen_s, tbl_s, wq_s, pen_s, sc_s, s0_s, obr_s, opt_s, *, s_len):
    i = pl.program_id(0)
    q0 = i * Q_BLOCK
    nc = s_len // CMP_BLOCK
    nch = nc // 2
    nblk = s_len // SEL_BLOCK
    k_top = min(TOP_N, nblk)
    wrows = min(32, nch)
    wk_refs = (wk0, wk1, wk2, wk3, wk4)
    wv_refs = (wv0, wv1, wv2, wv3, wv4)
    n_win = len(wk_refs)

    def table(h):
        return lambda b: (bias_ref[b, h] - bias_ref[N_BUCKETS - 1, h]) * LOG2E

    @pl.when(i == 0)
    def _():
        wq_s[...] = jnp.zeros_like(wq_s)
        rho = lax.broadcasted_iota(jnp.int32, (NEAR_ROWS, Q_BLOCK), 0)
        a = lax.broadcasted_iota(jnp.int32, (NEAR_ROWS, Q_BLOCK), 1)
        dist = a + (MAX_DISTANCE + NEAR_PAD) - rho
        for h in range(N_HEADS):
            nb_s[h] = _bias_from_dist(dist, table(h))
        loc_w = lax.broadcasted_iota(jnp.int32, (n_win * Q_BLOCK, Q_BLOCK), 0)
        d_w = lax.broadcasted_iota(jnp.int32, (n_win * Q_BLOCK, Q_BLOCK), 1) + WINDOW - loc_w
        wpen_s[...] = jnp.where((d_w >= 0) & (d_w < WINDOW), 0.0, NEG_INF)
        t = tblt_ref[...]
        ts = (t - t[:, N_BUCKETS - 1:N_BUCKETS]) * LOG2E
        for h in range(N_HEADS):
            tbl_s[h] = jnp.broadcast_to(ts[h:h + 1, :], (8, 128))

    for g in range(N_KV):
        off = (g % 2) * HEAD_DIM
        for r in range(Q_PER_KV):
            wq_s[g, off:off + HEAD_DIM, r * Q_BLOCK:(r + 1) * Q_BLOCK] = qt_ref[g, r * HEAD_DIM:(r + 1) * HEAD_DIM, :]

    lane_q = lax.broadcasted_iota(jnp.int32, (1, LANES_G), 1) % Q_BLOCK
    t_lane = q0 + lane_q
    blk = lax.broadcasted_iota(jnp.int32, (nblk, Q_BLOCK), 0)
    tq = lax.broadcasted_iota(jnp.int32, (nblk, Q_BLOCK), 1) + q0
    forced = (blk == 0) | (blk == tq // SEL_BLOCK)
    valid = blk * SEL_BLOCK <= tq
    blk_f = blk.astype(F32)

    w0 = jnp.clip(((2 * i - 16) // 8) * 8, 0, nch - wrows)
    w0 = pl.multiple_of(w0, 8)
    jj = lax.broadcasted_iota(jnp.int32, (wrows, Q_BLOCK), 0) + w0
    aq = lax.broadcasted_iota(jnp.int32, (wrows, Q_BLOCK), 1) + q0
    buckets = []
    for half in range(2):
        dist = aq - SEL_BLOCK * jj - (CMP_BLOCK * half + CMP_BLOCK - 1)
        bucket = jnp.zeros((wrows, Q_BLOCK), jnp.int32)
        for lo in _BUCKET_LOWS:
            bucket = bucket + (dist >= lo).astype(jnp.int32)
        buckets.append(bucket)

    def group_body(g, carry):
        wq = wq_s[g]
        pair = g // 2

        sc_s[...] = jnp.dot(kck_ref[pair], wq, preferred_element_type=F32)
        for r in range(Q_PER_KV):
            trow = jnp.concatenate([tbl_s[g * Q_PER_KV + r]] * (wrows // 8), axis=0)
            for half in range(2):
                rows = pl.ds(pl.multiple_of(half * nch + w0, 8), wrows)
                sc_s[rows, r * Q_BLOCK:(r + 1) * Q_BLOCK] += jnp.take_along_axis(trow, buckets[half], axis=1)
        rho = lax.broadcasted_iota(jnp.int32, (nc, 1), 0)
        ends = jnp.where(rho >= nch, SEL_BLOCK * (rho - nch) + 2 * CMP_BLOCK - 1, SEL_BLOCK * rho + CMP_BLOCK - 1)
        p_c = _softmax_rows(jnp.where(ends <= t_lane, sc_s[...], NEG_INF))
        obr_s[0, g] = jnp.dot(kcvt_ref[g], p_c.astype(BF16), preferred_element_type=F32)

        psum = p_c[:, 0:Q_BLOCK]
        for r in range(1, Q_PER_KV):
            psum = psum + p_c[:, r * Q_BLOCK:(r + 1) * Q_BLOCK]
        imp = psum[0:nch] + psum[nch:nc]
        pen_s[g, :, 0:Q_BLOCK] = jnp.where(forced, jnp.inf, jnp.where(valid, imp, NEG_INF))

        kw = jnp.concatenate([wk_refs[k][pair] for k in range(n_win)], axis=0)
        s = jnp.dot(kw, wq, preferred_element_type=F32)
        nw = n_win * Q_BLOCK
        r_w = NEAR_PAD + MAX_DISTANCE - WINDOW
        band = jnp.concatenate(
            [wpen_s[k * Q_BLOCK:(k + 1) * Q_BLOCK, :] + jnp.where(i - (n_win - 1) + k >= 0, 0.0, NEG_INF)
             for k in range(n_win)], axis=0)
        bias = jnp.concatenate([nb_s[g * Q_PER_KV + r, r_w:r_w + nw, :] + band for r in range(Q_PER_KV)], axis=1)
        p_w = _softmax_rows(s + bias)
        vw = jnp.concatenate([wv_refs[k][g] for k in range(n_win)], axis=1)
        obr_s[2, g] = jnp.dot(vw, p_w.astype(BF16), preferred_element_type=F32)
        return carry

    lax.fori_loop(0, N_KV, group_body, 0)

    def pick(_, c):
        for g in range(N_KV):
            sc = pen_s[g, :, 0:Q_BLOCK]
            m = jnp.max(sc, axis=0, keepdims=True)
            idx = jnp.min(jnp.where(sc == m, blk_f, float(nblk)), axis=0, keepdims=True)
            pen_s[g, :, 0:Q_BLOCK] = jnp.where(blk_f == idx, NEG_INF, sc)
        return c

    lax.fori_loop(0, k_top, pick, 0)
    for g in range(N_KV):
        picked = (pen_s[g, :, 0:Q_BLOCK] == NEG_INF) & (forced | valid)
        pen = jnp.where(picked, 0.0, NEG_INF)
        pen_s[g] = jnp.concatenate([pen] * Q_PER_KV, axis=1)

    n_tiles = (q0 + Q_BLOCK - 1) // KEY_TILE + 1
    n_far = jnp.maximum(q0 - MAX_DISTANCE, 0) // KEY_TILE
    nbt = KEY_TILE // SEL_BLOCK

    def qk(g, tt):
        k0 = pl.multiple_of(tt * KEY_TILE, KEY_TILE)
        return jnp.dot(selk_ref[g // 2, pl.ds(k0, KEY_TILE), :], wq_s[g], preferred_element_type=F32)

    def tile(tt, state, kind):
        k0 = pl.multiple_of(tt * KEY_TILE, KEY_TILE)
        if kind != "far":
            r0 = pl.multiple_of(jnp.maximum(k0 - q0 + MAX_DISTANCE + NEAR_PAD, 0), Q_BLOCK)
        if kind == "diag":
            causal = k0 + lax.broadcasted_iota(jnp.int32, (KEY_TILE, 1), 0) <= t_lane
        out = []
        s_next = s0_s[...]
        for g in range(N_KV):
            s = s_next
            if g + 1 < N_KV:
                s_next = qk(g + 1, tt)
            elif kind != "diag":
                s_next = qk(0, tt + 1)
            m_old, acc = state[g]
            pen_t = pen_s[g, pl.ds(pl.multiple_of(tt * nbt, nbt), nbt), :]
            if kind != "far":
                s = s + jnp.concatenate(
                    [nb_s[g * Q_PER_KV + r, pl.ds(r0, KEY_TILE), :] for r in range(Q_PER_KV)], axis=1)
            if kind == "diag":
                s = jnp.where(causal, s, NEG_INF)
            else:
                s = s.astype(BF16)
                pen_t = pen_t.astype(BF16)
            s = (s.reshape(nbt, SEL_BLOCK, LANES_G) + pen_t[:, None, :]).reshape(KEY_TILE, LANES_G)
            m_new = jnp.maximum(m_old, jnp.max(s, axis=0, keepdims=True).astype(F32))
            alpha = jnp.exp2(m_old - m_new)
            p = jnp.exp2(s - m_new.astype(s.dtype)).astype(BF16)
            pv = jnp.dot(selvt_ref[g, :, pl.ds(k0, KEY_TILE)], p, preferred_element_type=F32)
            out.append((m_new, alpha * acc + pv))
        if kind != "diag":
            s0_s[...] = s_next
        return tuple(out)

    s0_s[...] = qk(0, 0)
    init = tuple((jnp.full((1, LANES_G), M_FLOOR, F32), jnp.zeros((V_ROWS, LANES_G), F32)) for _ in range(N_KV))
    far_done = lax.fori_loop(0, n_far, lambda tt, c: tile(tt, c, "far"), init)
    mid_done = lax.fori_loop(n_far, n_tiles - 1, lambda tt, c: tile(tt, c, "mid"), far_done)
    sel_done = tile(n_tiles - 1, mid_done, "diag")
    for g in range(N_KV):
        acc = sel_done[g][1]
        obr_s[1, g] = acc[0:HEAD_DIM] / acc[HEAD_DIM:HEAD_DIM + 1]

    for g in range(N_KV):
        for r in range(Q_PER_KV):
            h = g * Q_PER_KV + r
            sl = slice(r * Q_BLOCK, (r + 1) * Q_BLOCK)
            merged = (gt_ref[h:h + 1, :] * obr_s[0, g, :, sl] + gt_ref[N_HEADS + h:N_HEADS + h + 1, :] * obr_s[1, g, :, sl]
                      + gt_ref[2 * N_HEADS + h:2 * N_HEADS + h + 1, :] * obr_s[2, g, :, sl])
            opt_s[h * HEAD_DIM:(h + 1) * HEAD_DIM, :] = merged
    op = opt_s[...].T
    y = jnp.dot((op * _silu(u_ref[...])).astype(BF16), wout_ref[...], preferred_element_type=F32)
    y_ref[...] = _rms(x1_ref[...] + y, gfin_ref[...])


def _attn_prompt(rel_bias, qt, kck, kcvt, selk, selvt, wink, winvt, gt, u, x1, w_out, gfin):
    s = x1.shape[0]
    nq = s // Q_BLOCK
    nc = s // CMP_BLOCK
    nblk = s // SEL_BLOCK
    assert s % (2 * CMP_BLOCK * 128) == 0 and nblk >= TOP_N
    n_win = WINDOW // Q_BLOCK + 1
    full = lambda shp, **kw: pl.BlockSpec(shp, lambda i: (0,) * len(shp), **kw)
    once = dict(pipeline_mode=pl.Buffered(1))
    win_k_specs = [pl.BlockSpec((2, Q_BLOCK, 128), functools.partial(lambda i, k: (0, jnp.maximum(i - (n_win - 1) + k, 0), 0), k=k))
                   for k in range(n_win)]
    win_v_specs = [pl.BlockSpec((N_KV, HEAD_DIM, Q_BLOCK), functools.partial(lambda i, k: (0, 0, jnp.maximum(i - (n_win - 1) + k, 0)), k=k))
                   for k in range(n_win)]
    return pl.pallas_call(
        functools.partial(_attn_prompt_kernel, s_len=s),
        grid=(nq,),
        in_specs=[pl.BlockSpec(memory_space=pltpu.SMEM), full((N_HEADS, 128)),
                  pl.BlockSpec((N_KV, 256, Q_BLOCK), lambda i: (0, 0, i)),
                  full((2, nc, 128), **once), full((N_KV, HEAD_DIM, nc), **once),
                  full((2, s, 128), **once), full((N_KV, V_ROWS, s), **once),
                  *win_k_specs, *win_v_specs,
                  pl.BlockSpec((GATE_PAD, Q_BLOCK), lambda i: (0, i)),
                  pl.BlockSpec((Q_BLOCK, 1024), lambda i: (i, 0)),
                  pl.BlockSpec((Q_BLOCK, D_MODEL), lambda i: (i, 0)),
                  full((1024, D_MODEL), **once), full((1, D_MODEL))],
        out_specs=pl.BlockSpec((Q_BLOCK, D_MODEL), lambda i: (i, 0)),
        out_shape=jax.ShapeDtypeStruct((s, D_MODEL), F32),
        scratch_shapes=[pltpu.VMEM((N_HEADS, NEAR_ROWS, Q_BLOCK), F32),
                        pltpu.VMEM((n_win * Q_BLOCK, Q_BLOCK), F32),
                        pltpu.VMEM((N_HEADS, 8, 128), F32),
                        pltpu.VMEM((N_KV, 128, LANES_G), BF16),
                        pltpu.VMEM((N_KV, nblk, LANES_G), F32),
                        pltpu.VMEM((nc, LANES_G), F32),
                        pltpu.VMEM((KEY_TILE, LANES_G), F32),
                        pltpu.VMEM((3, N_KV, HEAD_DIM, LANES_G), F32),
                        pltpu.VMEM((N_HEADS * HEAD_DIM, Q_BLOCK), F32)],
        compiler_params=_cparams("arbitrary"),
        name="nsa_attn_prompt",
    )(rel_bias, jnp.pad(rel_bias.T, ((0, 0), (0, 128 - N_BUCKETS))), qt, kck, kcvt, selk, selvt,
      *([wink] * n_win), *([winvt] * n_win), gt, u, x1, w_out, gfin)


PAGES_PER_STEP = 16
CHUNK_PAGES = 32
STEPS_PER_CHUNK = CHUNK_PAGES // PAGES_PER_STEP
NEAR_PAGES = MAX_DISTANCE // PAGE_SIZE
NEW_PAD = 128


def _proj_sample_kernel(x_ref, g_ref, wn_ref, kvcs_ref, kvw_ref, q_ref, u_ref, gate_ref):
    xn = _rms(x_ref[...], g_ref[...]).astype(BF16)
    z = jnp.dot(xn, wn_ref[...], preferred_element_type=F32)
    kvcs_ref[...] = z[:, 0:1024]
    kvw_ref[...] = z[:, 1024:1536]
    q_ref[...] = (z[:, 1536:2560] * (HEAD_DIM ** -0.5)).astype(BF16)
    u_ref[...] = z[:, 2560:3584]
    gate_ref[...] = _sigmoid(z[:, 3584:3584 + GATE_PAD])


def _proj_sample(x1, g, w_nat):
    rows = x1.shape[0]
    sds = jax.ShapeDtypeStruct
    return pl.pallas_call(
        _proj_sample_kernel,
        out_shape=[sds((rows, 1024), F32), sds((rows, 512), F32), sds((rows, 1024), BF16),
                   sds((rows, 1024), F32), sds((rows, GATE_PAD), F32)],
        compiler_params=pltpu.CompilerParams(vmem_limit_bytes=VMEM_LIMIT),
        name="nsa_proj_sample",
    )(x1, g, w_nat)


def _softmax_lanes_online(s, m_old, l_old):
    m_new = jnp.maximum(m_old, jnp.max(s, axis=1, keepdims=True))
    alpha = jnp.exp(m_old - m_new)
    p = jnp.exp(s - m_new)
    return m_new, alpha, alpha * l_old + jnp.sum(p, axis=1, keepdims=True), p


def _attn_sample_kernel(pt_ref, *refs, past, ds, wb):
    pages = refs[:PAGES_PER_STEP]
    (q_ref, new_ref, wint_ref, gate_ref, u_ref, x1_ref, perm_ref, wk_ref, wv_ref, tb_ref, wout_ref, gfin_ref, y_ref,
     xl_s, kckt_s, kcv_s, skt_s, svt_s, bc_s, bs_s, bnew_s, bw_s, obr_s) = refs[PAGES_PER_STEP:]
    b = pl.program_id(0)
    pg = pl.program_id(1)
    npg = pl.num_programs(1)
    rows = Q_PER_KV * N_KV * ds
    nchunk = past // (CHUNK_PAGES * PAGE_SIZE)
    npages = past // PAGE_SIZE
    ncl = past // CMP_BLOCK
    nblk = past // SEL_BLOCK
    k_top = min(TOP_N - 1, nblk)
    nt = (((1,), (1,)), ((), ()))

    row_i = lax.broadcasted_iota(jnp.int32, (rows, 1), 0)
    tt = row_i % ds

    @pl.when((b == 0) & (pg == 0))
    def _():
        tbs = tb_ref[...] - tb_ref[:, N_BUCKETS - 1:N_BUCKETS]
        col = lambda k: tbs[:, k:k + 1]
        lane = lax.broadcasted_iota(jnp.int32, (1, ncl), 1)
        slot = lane % 8
        cblk = 8 * (lane // 8) + jnp.where(slot < 4, 2 * slot, 2 * (slot - 4) + 1)
        bc_s[...] = _bias_from_dist(past + tt - (CMP_BLOCK * cblk + CMP_BLOCK - 1), col)
        lane_p = lax.broadcasted_iota(jnp.int32, (1, PAGE_SIZE), 1)
        for k in range(NEAR_PAGES):
            bs_s[k] = _bias_from_dist(tt + MAX_DISTANCE - PAGE_SIZE * k - lane_p, col)
        bnew_s[...] = _bias_from_dist(tt - lax.broadcasted_iota(jnp.int32, (1, NEW_PAD), 1), col)
        bw_s[...] = _bias_from_dist(tt + wb - lax.broadcasted_iota(jnp.int32, (1, wb), 1), col)

    for pair in range(PAGES_PER_STEP // 2):
        pa, pb = pages[2 * pair], pages[2 * pair + 1]
        slab = pl.ds(pl.multiple_of(((pg % STEPS_PER_CHUNK) * (PAGES_PER_STEP // 2) + pair) * 8, 8), 8)
        for kind in range(2):
            xt2 = jnp.concatenate([pa[0, kind * 256:(kind + 1) * 256, :], pb[0, kind * 256:(kind + 1) * 256, :]],
                                  axis=1).astype(BF16)
            x_perm = lax.dot_general(perm_ref[...], xt2, nt, preferred_element_type=F32)
            xl_s[:, slab, kind * 256:(kind + 1) * 256] = x_perm.reshape(CMP_BLOCK, 8, 256)
    for k in range(PAGES_PER_STEP):
        skt_s[pg * PAGES_PER_STEP + k] = pages[k][0, 512:768, :].astype(BF16)
        svt_s[pg * PAGES_PER_STEP + k] = pages[k][0, 768:1024, :].astype(BF16)

    @pl.when(pg % STEPS_PER_CHUNK == STEPS_PER_CHUNK - 1)
    def _():
        nbc = CHUNK_PAGES * PAGE_SIZE // CMP_BLOCK

        def body(l, acc):
            ak, av = acc
            xk = xl_s[l, :, 0:256].astype(BF16)
            xv = xl_s[l, :, 256:512].astype(BF16)
            ak = ak + jnp.dot(xk, wk_ref[l], preferred_element_type=F32)
            av = av + jnp.dot(xv, wv_ref[l], preferred_element_type=F32)
            return ak, av

        ak, av = lax.fori_loop(0, CMP_BLOCK, body, (jnp.zeros((nbc, 256), F32), jnp.zeros((nbc, 256), F32)),
                               unroll=4)
        ch = pg // STEPS_PER_CHUNK
        kckt_s[ch] = ak.T.astype(BF16)
        kcv_s[ch] = av.astype(BF16)

    @pl.when(pg == npg - 1)
    def _():
        lane_g = lax.broadcasted_iota(jnp.int32, (1, 256), 1) // HEAD_DIM
        row_g = (row_i // ds) % N_KV
        wq = jnp.where(row_g == lane_g, q_ref[0], jnp.zeros((), BF16))
        t_row = past + tt

        s_c = jnp.concatenate([jnp.dot(wq, kckt_s[ch], preferred_element_type=F32) for ch in range(nchunk)],
                              axis=1) + bc_s[...]
        m = jnp.max(s_c, axis=1, keepdims=True)
        e = jnp.exp(s_c - m)
        p_c = e / jnp.sum(e, axis=1, keepdims=True)
        o_c = jnp.zeros((rows, 256), F32)
        for ch in range(nchunk):
            o_c = o_c + jnp.dot(p_c[:, ch * 128:(ch + 1) * 128].astype(BF16), kcv_s[ch], preferred_element_type=F32)
        obr_s[0] = o_c

        gt_rows = N_KV * ds
        ps = p_c[0:gt_rows]
        for r in range(1, Q_PER_KV):
            ps = ps + p_c[r * gt_rows:(r + 1) * gt_rows]
        imp = ps + pltpu.roll(ps, ncl - 4, axis=1)
        lane = lax.broadcasted_iota(jnp.int32, (gt_rows, ncl), 1)
        slot = lane % 8
        jblk = 4 * (lane // 8) + slot
        t_gt = past + lax.broadcasted_iota(jnp.int32, (gt_rows, ncl), 0) % ds
        forced = (jblk == 0) | (jblk == t_gt // SEL_BLOCK)
        valid = jblk * SEL_BLOCK <= t_gt
        scores = jnp.where(slot < 4, jnp.where(forced, jnp.inf, jnp.where(valid, imp, NEG_INF)), NEG_INF)
        jf = jnp.where(slot < 4, jblk, 2 * nblk).astype(F32)
        picks = []
        for _ in range(k_top):
            mx = jnp.max(scores, axis=1, keepdims=True)
            idx = jnp.min(jnp.where(scores == mx, jf, float(2 * nblk)), axis=1, keepdims=True)
            scores = jnp.where(jf == idx, NEG_INF, scores)
            picks.append(idx)

        gk = NEAR_PAGES
        blk_lane = (lax.broadcasted_iota(jnp.int32, (1, gk * PAGE_SIZE), 1) // SEL_BLOCK).astype(F32)

        def group_logits(pgi):
            return jnp.concatenate([jnp.dot(wq, skt_s[pgi * gk + k], preferred_element_type=F32) for k in range(gk)],
                                   axis=1)

        def pages_step(pgi, carry, bias, s=None):
            m_old, l_old, acc = carry
            if s is None:
                s = group_logits(pgi)
            if bias is not None:
                s = s + bias
            jk = blk_lane + jnp.asarray(pgi * (gk * PAGE_SIZE // SEL_BLOCK), F32)
            hit = jk == picks[0]
            for pk in picks[1:]:
                hit = hit | (jk == pk)
            pen = jnp.where(hit, 0.0, NEG_INF)
            s = (s.reshape(Q_PER_KV, gt_rows, gk * PAGE_SIZE) + pen[None]).reshape(rows, gk * PAGE_SIZE)
            m_new, alpha, l_new, pr = _softmax_lanes_online(s, m_old, l_old)
            prb = pr.astype(BF16)
            pv = alpha * acc
            for k in range(gk):
                pv = pv + lax.dot_general(prb[:, k * PAGE_SIZE:(k + 1) * PAGE_SIZE], svt_s[pgi * gk + k], nt,
                                          preferred_element_type=F32)
            return m_new, l_new, pv

        empty = (jnp.full((rows, 1), M_FLOOR, F32), jnp.zeros((rows, 1), F32), jnp.zeros((rows, 256), F32))
        n_far_groups = npages // gk - 1
        n_pairs = n_far_groups // 2

        def pair_step(j, chains):
            ca, cb = chains
            sa, sb = group_logits(j), group_logits(j + n_pairs)
            return pages_step(j, ca, None, sa), pages_step(j + n_pairs, cb, None, sb)

        ca, cb = lax.fori_loop(0, n_pairs, pair_step, (empty, empty))
        for j in range(2 * n_pairs, n_far_groups):
            ca = pages_step(j, ca, None)
        m_far = jnp.maximum(ca[0], cb[0])
        wa, wb_ = jnp.exp(ca[0] - m_far), jnp.exp(cb[0] - m_far)
        carry = (m_far, wa * ca[1] + wb_ * cb[1], wa * ca[2] + wb_ * cb[2])
        bias_near = jnp.concatenate([bs_s[k] for k in range(NEAR_PAGES)], axis=1)
        m_old, l_old, acc = pages_step(n_far_groups, carry, bias_near)
        lane_n = lax.broadcasted_iota(jnp.int32, (1, NEW_PAD), 1)
        ok_new = (lane_n < ds) & (lane_n <= tt)
        s = lax.dot_general(wq, new_ref[0, 0], nt, preferred_element_type=F32) + bnew_s[...]
        m_new, alpha, l_new, pr = _softmax_lanes_online(jnp.where(ok_new, s, NEG_INF), m_old, l_old)
        acc = alpha * acc + jnp.dot(pr.astype(BF16), new_ref[0, 1], preferred_element_type=F32)
        obr_s[1] = acc / l_new

        kwt = wint_ref[0, 0:256, :].astype(BF16)
        vwt = wint_ref[0, 256:512, :].astype(BF16)
        dist = tt + wb - lax.broadcasted_iota(jnp.int32, (1, wb), 1)
        s_w = jnp.where((dist >= 0) & (dist < WINDOW), jnp.dot(wq, kwt, preferred_element_type=F32) + bw_s[...], NEG_INF)
        s_n = jnp.where(ok_new, lax.dot_general(wq, new_ref[0, 2], nt, preferred_element_type=F32) + bnew_s[...], NEG_INF)
        m = jnp.maximum(jnp.max(s_w, axis=1, keepdims=True), jnp.max(s_n, axis=1, keepdims=True))
        e_w = jnp.exp(s_w - m)
        e_n = jnp.exp(s_n - m)
        den = jnp.sum(e_w, axis=1, keepdims=True) + jnp.sum(e_n, axis=1, keepdims=True)
        o_w = (lax.dot_general(e_w.astype(BF16), vwt, nt, preferred_element_type=F32)
               + jnp.dot(e_n.astype(BF16), new_ref[0, 3], preferred_element_type=F32))
        obr_s[2] = o_w / den

        merged = []
        for r in range(Q_PER_KV):
            tot = None
            for br in range(3):
                o_r = jnp.zeros((ds, 256), F32)
                g_r = jnp.zeros((ds, 256), F32)
                for g in range(N_KV):
                    base = r * gt_rows + g * ds
                    o_r = jnp.where(lane_g == g, obr_s[br, base:base + ds, :], o_r)
                    c = br * N_HEADS + g * Q_PER_KV + r
                    g_r = jnp.where(lane_g == g, gate_ref[0, :, c:c + 1], g_r)
                tot = g_r * o_r if tot is None else tot + g_r * o_r
            merged.append(tot)
        op = jnp.concatenate(merged, axis=1)
        y = jnp.dot((op * _silu(u_ref[0])).astype(BF16), wout_ref[...], preferred_element_type=F32)
        y_ref[0] = _rms(x1_ref[0] + y, gfin_ref[...])


def _attn_sample(page_table, pool_t, q64, newkv, win_t, gates, u_r, x1, perm, wkt_bd, wv_bd, tb_rows, w_out_r, gfin,
                 *, past, ds, wb):
    nb = page_table.shape[0]
    npages = past // PAGE_SIZE
    assert past % (CHUNK_PAGES * PAGE_SIZE) == 0 and npages >= NEAR_PAGES and past // SEL_BLOCK >= TOP_N
    npg = npages // PAGES_PER_STEP
    nchunk = npages // CHUNK_PAGES
    rows = Q_PER_KV * N_KV * ds
    ncl = past // CMP_BLOCK
    once = dict(pipeline_mode=pl.Buffered(1))
    full = lambda shp, **kw: pl.BlockSpec(shp, lambda b, p, pt: (0,) * len(shp), **kw)
    per_b = lambda shp: pl.BlockSpec((1,) + shp, lambda b, p, pt: (b,) + (0,) * len(shp))
    page_specs = [pl.BlockSpec((1, 1024, PAGE_SIZE),
                               functools.partial(lambda b, p, pt, k: (pt[b, p * PAGES_PER_STEP + k], 0, 0), k=k))
                  for k in range(PAGES_PER_STEP)]
    grid_spec = pltpu.PrefetchScalarGridSpec(
        num_scalar_prefetch=1, grid=(nb, npg),
        in_specs=[*page_specs, per_b((rows, 256)), per_b((4, NEW_PAD, 256)), per_b((512, wb)),
                  per_b((ds, GATE_PAD)), per_b((ds, 1024)), per_b((ds, D_MODEL)),
                  full((256, 256), **once), full((CMP_BLOCK, 256, 256), **once), full((CMP_BLOCK, 256, 256), **once),
                  full((rows, N_BUCKETS)), full((1024, D_MODEL), **once), full((1, D_MODEL))],
        out_specs=pl.BlockSpec((1, ds, D_MODEL), lambda b, p, pt: (b, 0, 0)),
        scratch_shapes=[pltpu.VMEM((CMP_BLOCK, CHUNK_PAGES * PAGE_SIZE // CMP_BLOCK, 512), F32),
                        pltpu.VMEM((nchunk, 256, 128), BF16), pltpu.VMEM((nchunk, 128, 256), BF16),
                        pltpu.VMEM((npages, 256, PAGE_SIZE), BF16), pltpu.VMEM((npages, 256, PAGE_SIZE), BF16),
                        pltpu.VMEM((rows, ncl), F32), pltpu.VMEM((NEAR_PAGES, rows, PAGE_SIZE), F32),
                        pltpu.VMEM((rows, NEW_PAD), F32), pltpu.VMEM((rows, wb), F32),
                        pltpu.VMEM((3, rows, 256), F32)])
    return pl.pallas_call(
        functools.partial(_attn_sample_kernel, past=past, ds=ds, wb=wb),
        grid_spec=grid_spec,
        out_shape=jax.ShapeDtypeStruct((nb, ds, D_MODEL), F32),
        compiler_params=_cparams("arbitrary", "arbitrary"),
        name="nsa_attn_sample",
    )(page_table, *([pool_t] * PAGES_PER_STEP), q64, newkv, win_t, gates, u_r, x1, perm, wkt_bd, wv_bd, tb_rows,
      w_out_r, gfin)


def _pair_permutation():
    m = jnp.arange(256)
    l, slot = m // 8, m % 8
    cb = jnp.where(slot < 4, 2 * slot, 2 * (slot - 4) + 1)
    tok = CMP_BLOCK * cb + l
    return (tok[:, None] == jnp.arange(256)[None, :]).astype(BF16)


def _block_diag4(w):
    eye = jnp.eye(N_KV, dtype=w.dtype)
    out = eye[:, None, :, None] * w[..., None, :, None, :]
    return out.reshape(*w.shape[:-2], N_KV * HEAD_DIM, N_KV * HEAD_DIM)


def kernel(x_prompt, x_sample, state_lru_h, state_conv, cache_nsa_kv, cache_win_kv, page_table, norm_g,
           final_norm_g, w_in_lru, conv_w, conv_b, w_gate_a, b_gate_a, w_gate_x, b_gate_x, lru_lambda,
           w_out_lru, w_in_nsa, w_cmp, w_out_nsa, rel_bias):
    bp, s_len, _ = x_prompt.shape
    nb, ds, _ = x_sample.shape
    assert bp == 1
    row = lambda v: v.reshape(1, -1)

    lru_w = (row(norm_g[0]), w_in_lru[0].astype(BF16), conv_w[0], row(conv_b[0]), w_gate_a[0].astype(BF16),
             row(b_gate_a[0]), w_gate_x[0].astype(BF16), row(b_gate_x[0]), row(lru_lambda[0]),
             w_out_lru[0].astype(BF16))
    x1p, hp, tailp = _lru_prompt(x_prompt[0], *lru_w)
    xs_tm = x_sample.transpose(1, 0, 2).reshape(ds * nb, D_MODEL)
    buf_tm = state_conv[0].transpose(1, 0, 2).reshape((CONV_W - 1) * nb, D_MODEL)
    x1s_tm, hs_last, tails = _lru_sample(xs_tm, buf_tm, state_lru_h[0], *lru_w, nb=nb, ds=ds)

    prompt_lru_h = hp.reshape(1, 1, D_MODEL)
    prompt_conv = tailp[8 - (CONV_W - 1):].reshape(1, 1, CONV_W - 1, D_MODEL)
    sample_lru_h = hs_last.reshape(1, nb, D_MODEL)
    sample_conv = tails.reshape(CONV_W - 1, nb, D_MODEL).transpose(1, 0, 2)[None]

    w = w_in_nsa[0]
    wq, wkv = w[:, :1024], w[:, 1024:1024 + KV_WIDTH]
    wg, wu = w[:, 1024 + KV_WIDTH:1024 + KV_WIDTH + 3 * N_HEADS], w[:, 1024 + KV_WIDTH + 3 * N_HEADS:]
    wg_pad = jnp.pad(wg, ((0, 0), (0, GATE_PAD - 3 * N_HEADS)))
    w_nat_p = jnp.concatenate([wkv[:, 0:768], wkv[:, 1024:1280], wu], axis=1).astype(BF16)
    w_t_p = jnp.concatenate([wq.T, wkv.T, wg_pad.T], axis=0).astype(BF16)
    wk_bd = _block_diag4(w_cmp[0, 0]).astype(BF16)
    wvt_bd = jnp.swapaxes(_block_diag4(w_cmp[0, 1]), -1, -2).astype(BF16)
    w_out = w_out_nsa[0].astype(BF16)
    g1 = row(norm_g[1])
    gfin = row(final_norm_g)

    cmp_kv, selk, wink, u_p, qt, kvcst, kvwt, selvt, winvt, gt = _proj_prompt(x1p, g1, w_nat_p, w_t_p)
    kck, kcvt = _compress_prompt(cmp_kv, wk_bd, wvt_bd)
    y_prompt = _attn_prompt(rel_bias, qt, kck, kcvt, selk, selvt, wink, winvt, gt, u_p, x1p, w_out, gfin)
    y_prompt = y_prompt[None]
    prompt_kv_rows = kvcst.reshape(4, N_KV, HEAD_DIM, s_len).transpose(3, 0, 1, 2)[None, None]
    wbp = min(WINDOW, s_len)
    prompt_win_kv = kvwt[:, s_len - wbp:].reshape(2, N_KV, HEAD_DIM, wbp).transpose(3, 0, 1, 2)[None, None]

    n_phys, n_layers = cache_nsa_kv.shape[:2]
    past = page_table.shape[1] * PAGE_SIZE
    wb = cache_win_kv.shape[2]
    to_rgd = lambda m: m.reshape(m.shape[0], N_KV, Q_PER_KV, HEAD_DIM).transpose(0, 2, 1, 3).reshape(m.shape[0], 1024)
    w_nat_s = jnp.concatenate([wkv, to_rgd(wq), to_rgd(wu), wg_pad], axis=1).astype(BF16)
    w_out_r = to_rgd(w_out_nsa[0].T).T.astype(BF16)
    x1s = x1s_tm.reshape(ds, nb, D_MODEL).transpose(1, 0, 2).reshape(nb * ds, D_MODEL)
    kvcs_s, kvw_s, q_s, u_s, gate_s = _proj_sample(x1s, g1, w_nat_s)
    q64 = q_s.reshape(nb, ds, Q_PER_KV, 1, 256).transpose(0, 2, 3, 1, 4)
    q64 = jnp.broadcast_to(q64, (nb, Q_PER_KV, N_KV, ds, 256)).reshape(nb, Q_PER_KV * N_KV * ds, 256)
    new4 = jnp.stack([kvcs_s[:, 512:768], kvcs_s[:, 768:1024], kvw_s[:, 0:256], kvw_s[:, 256:512]], axis=0)
    new4 = new4.reshape(4, nb, ds, 256).transpose(1, 0, 2, 3).astype(BF16)
    newkv = jnp.pad(new4, ((0, 0), (0, 0), (0, NEW_PAD - ds), (0, 0)))
    pool_t = cache_nsa_kv.transpose(0, 1, 3, 4, 5, 2).reshape(n_phys * n_layers, 1024, PAGE_SIZE)
    win_t = cache_win_kv[0].transpose(0, 2, 3, 4, 1).reshape(nb, 512, wb)
    head_of_row = (jnp.arange(Q_PER_KV * N_KV * ds) // ds % N_KV) * Q_PER_KV + jnp.arange(Q_PER_KV * N_KV * ds) // (N_KV * ds)
    tb_rows = rel_bias.T[head_of_row]
    y_sample = _attn_sample(page_table * n_layers, pool_t, q64, newkv, win_t, gate_s.reshape(nb, ds, GATE_PAD),
                            u_s.reshape(nb, ds, 1024), x1s.reshape(nb, ds, D_MODEL), _pair_permutation(),
                            wk_bd, jnp.swapaxes(wvt_bd, -1, -2), tb_rows, w_out_r, gfin,
                            past=past, ds=ds, wb=wb)
    sample_kv_rows = kvcs_s.reshape(nb, 1, ds, 4, N_KV, HEAD_DIM)
    new_t = kvw_s.reshape(nb, ds, 512).transpose(0, 2, 1)
    win_all = jnp.concatenate([win_t, new_t], axis=2)[:, :, ds:]
    sample_win_kv = win_all.reshape(nb, 2, N_KV, HEAD_DIM, wb).transpose(0, 4, 1, 2, 3)[None]

    return (y_prompt, y_sample, prompt_lru_h, prompt_conv, prompt_kv_rows, prompt_win_kv,
            sample_lru_h, sample_conv, sample_kv_rows, sample_win_kv)
```

```python
import functools
import math

import jax
import jax.numpy as jnp
from jax import lax
from jax.experimental import pallas as pl
from jax.experimental.pallas import tpu as pltpu

F32 = jnp.float32
BF16 = jnp.bfloat16

D_MODEL = 1024
HEAD_DIM = 64
N_HEADS = 16
N_KV = 4
Q_PER_KV = 4
CMP_BLOCK = 32
SEL_BLOCK = 64
TOP_N = 16
WINDOW = 512
PAGE_SIZE = 128
N_BUCKETS = 32
MAX_DISTANCE = 1024
N_LRU_BLOCKS = 8
LRU_BLOCK = 128
CONV_W = 4
LRU_C = 8.0
RMS_EPS = 1e-6

V7X_VMEM_BYTES = 64 * 1024 * 1024
VMEM_LIMIT = V7X_VMEM_BYTES - 8 * 1024 * 1024


def _bucket_lower_bounds():
    max_exact = N_BUCKETS // 2
    lows = list(range(1, max_exact + 1))
    for k in range(1, N_BUCKETS - max_exact):
        n = max_exact
        while n ** 8 < 2 ** (32 + 3 * k):
            n += 1
        lows.append(n)
    return lows


_BUCKET_LOWS = _bucket_lower_bounds()


def _bias_from_dist(dist, table_col):
    val = jnp.full(dist.shape, table_col(0), F32)
    for b, lo in enumerate(_BUCKET_LOWS, start=1):
        val = jnp.where(dist >= lo, table_col(b), val)
    return val


def _rms(x, g):
    ms = jnp.mean(x * x, axis=-1, keepdims=True)
    return x * lax.rsqrt(ms + RMS_EPS) * g


def _sigmoid(x):
    return 1.0 / (1.0 + jnp.exp(-x))


def _silu(x):
    return x * _sigmoid(x)


def _softplus(x):
    return jnp.maximum(x, 0.0) + jnp.log1p(jnp.exp(-jnp.abs(x)))


def _expm1(x):
    u = jnp.exp(x)
    um1 = u - 1.0
    y = um1 * x / jnp.log(u)
    y = jnp.where(u == 1.0, x, y)
    return jnp.where(um1 == -1.0, -1.0, y)


def _cparams(*sem):
    return pltpu.CompilerParams(dimension_semantics=sem, vmem_limit_bytes=VMEM_LIMIT)


def _lru_gate_terms(conv, wa_ref, ba, wx_ref, bx, lam):
    cb = conv.astype(BF16)
    r_parts, i_parts = [], []
    for n in range(N_LRU_BLOCKS):
        c = cb[:, n * LRU_BLOCK:(n + 1) * LRU_BLOCK]
        r_parts.append(jnp.dot(c, wa_ref[n], preferred_element_type=F32))
        i_parts.append(jnp.dot(c, wx_ref[n], preferred_element_type=F32))
    r = _sigmoid(jnp.concatenate(r_parts, axis=1) + ba)
    i = _sigmoid(jnp.concatenate(i_parts, axis=1) + bx)
    log_a = -LRU_C * r * _softplus(-lam)
    a = jnp.exp(log_a)
    u = jnp.sqrt(-_expm1(2.0 * log_a)) * i * conv
    return a, u


def _lru_prompt_kernel(x_ref, g_ref, win_ref, cw_ref, cb_ref, wa_ref, ba_ref, wx_ref, bx_ref,
                       lam_ref, wout_ref, x1_ref, hlast_ref, tail_ref,
                       xb_ext, a_s, u_s, h_c, *, tr):
    i = pl.program_id(0)

    @pl.when(i == 0)
    def _():
        xb_ext[0:8, :] = jnp.zeros((8, D_MODEL), F32)
        h_c[...] = jnp.zeros_like(h_c)

    x = x_ref[...]
    xn = _rms(x, g_ref[...])
    z = jnp.dot(xn.astype(BF16), win_ref[...], preferred_element_type=F32)
    xb = z[:, :D_MODEL]
    gate = z[:, D_MODEL:]
    xb_ext[8:8 + tr, :] = xb
    conv = cb_ref[...]
    for j in range(CONV_W):
        conv = conv + cw_ref[j:j + 1, :] * xb_ext[8 - (CONV_W - 1) + j:8 - (CONV_W - 1) + j + tr, :]
    a, u = _lru_gate_terms(conv, wa_ref, ba_ref[...], wx_ref, bx_ref[...], lam_ref[...])
    a_s[...] = a
    u_s[...] = u

    row = lax.broadcasted_iota(jnp.int32, (8, D_MODEL), 0)

    def body(k, h):
        base = pl.multiple_of(k * 8, 8)
        aa = a_s[pl.ds(base, 8), :]
        uu = u_s[pl.ds(base, 8), :]
        for s in (1, 2, 4):
            keep = row >= s
            us = jnp.where(keep, pltpu.roll(uu, s, axis=0), 0.0)
            as_ = jnp.where(keep, pltpu.roll(aa, s, axis=0), 1.0)
            uu = uu + aa * us
            aa = aa * as_
        hb = aa * h + uu
        u_s[pl.ds(base, 8), :] = hb
        return hb[7:8, :]

    h_last = lax.fori_loop(0, tr // 8, body, h_c[...])
    h_c[...] = h_last
    hlast_ref[...] = h_last
    tail_ref[...] = xb[tr - 8:, :]
    xb_ext[0:8, :] = xb[tr - 8:, :]
    y = u_s[...] * _silu(gate)
    out = jnp.dot(y.astype(BF16), wout_ref[...], preferred_element_type=F32)
    x1_ref[...] = x + out


def _lru_prompt(x, g, w_in, cw, cb, wa, ba, wx, bx, lam, w_out, tr=512):
    s = x.shape[0]
    tr = min(tr, s)
    assert s % tr == 0 and tr % 8 == 0
    full = lambda shp: pl.BlockSpec(shp, lambda i: (0,) * len(shp))
    return pl.pallas_call(
        functools.partial(_lru_prompt_kernel, tr=tr),
        grid=(s // tr,),
        in_specs=[pl.BlockSpec((tr, D_MODEL), lambda i: (i, 0)), full((1, D_MODEL)),
                  full((D_MODEL, 2 * D_MODEL)), full((CONV_W, D_MODEL)), full((1, D_MODEL)),
                  full((N_LRU_BLOCKS, LRU_BLOCK, LRU_BLOCK)), full((1, D_MODEL)),
                  full((N_LRU_BLOCKS, LRU_BLOCK, LRU_BLOCK)), full((1, D_MODEL)),
                  full((1, D_MODEL)), full((D_MODEL, D_MODEL))],
        out_specs=[pl.BlockSpec((tr, D_MODEL), lambda i: (i, 0)), full((1, D_MODEL)), full((8, D_MODEL))],
        out_shape=[jax.ShapeDtypeStruct((s, D_MODEL), F32), jax.ShapeDtypeStruct((1, D_MODEL), F32),
                   jax.ShapeDtypeStruct((8, D_MODEL), F32)],
        scratch_shapes=[pltpu.VMEM((tr + 8, D_MODEL), F32), pltpu.VMEM((tr, D_MODEL), F32),
                        pltpu.VMEM((tr, D_MODEL), F32), pltpu.VMEM((1, D_MODEL), F32)],
        compiler_params=_cparams("arbitrary"),
        name="lru_prompt",
    )(x, g, w_in, cw, cb, wa, ba, wx, bx, lam, w_out)


def _lru_sample_kernel(x_ref, buf_ref, h0_ref, g_ref, win_ref, cw_ref, cb_ref, wa_ref, ba_ref, wx_ref,
                       bx_ref, lam_ref, wout_ref, x1_ref, hlast_ref, tail_ref, xp_s, *, nb, ds):
    x = x_ref[...]
    xn = _rms(x, g_ref[...])
    z = jnp.dot(xn.astype(BF16), win_ref[...], preferred_element_type=F32)
    xb = z[:, :D_MODEL]
    gate = z[:, D_MODEL:]
    nbuf = (CONV_W - 1) * nb
    xp_s[0:nbuf, :] = buf_ref[...]
    xp_s[nbuf:, :] = xb
    conv = cb_ref[...]
    for j in range(CONV_W):
        conv = conv + cw_ref[j:j + 1, :] * xp_s[j * nb:j * nb + ds * nb, :]
    a, u = _lru_gate_terms(conv, wa_ref, ba_ref[...], wx_ref, bx_ref[...], lam_ref[...])
    h = h0_ref[...]
    hs = []
    for t in range(ds):
        h = a[t * nb:(t + 1) * nb, :] * h + u[t * nb:(t + 1) * nb, :]
        hs.append(h)
    hlast_ref[...] = h
    tail_ref[...] = xp_s[ds * nb:, :]
    y = jnp.concatenate(hs, axis=0) * _silu(gate)
    out = jnp.dot(y.astype(BF16), wout_ref[...], preferred_element_type=F32)
    x1_ref[...] = x + out


def _lru_sample(x_tm, buf_tm, h0, g, w_in, cw, cb, wa, ba, wx, bx, lam, w_out, nb, ds):
    rows = nb * ds
    nbuf = (CONV_W - 1) * nb
    return pl.pallas_call(
        functools.partial(_lru_sample_kernel, nb=nb, ds=ds),
        out_shape=[jax.ShapeDtypeStruct((rows, D_MODEL), F32), jax.ShapeDtypeStruct((nb, D_MODEL), F32),
                   jax.ShapeDtypeStruct((nbuf, D_MODEL), F32)],
        scratch_shapes=[pltpu.VMEM((nbuf + rows, D_MODEL), F32)],
        compiler_params=pltpu.CompilerParams(vmem_limit_bytes=VMEM_LIMIT),
        name="lru_sample",
    )(x_tm, buf_tm, h0, g, w_in, cw, cb, wa, ba, wx, bx, lam, w_out)


Q_BLOCK = 128
LANES_G = Q_PER_KV * Q_BLOCK
KEY_TILE = 512
NEAR_PAD = KEY_TILE
NEAR_ROWS = NEAR_PAD + MAX_DISTANCE + KEY_TILE
NEG_INF = float("-inf")
LOG2E = math.log2(math.e)
M_FLOOR = -1e30

KV_WIDTH = 3 * 2 * N_KV * HEAD_DIM
GATE_PAD = 128
V_ROWS = HEAD_DIM + 16


def _proj_prompt_kernel(x_ref, g_ref, wn_ref, wt_ref, cmp_ref, selk_ref, wink_ref, u_ref,
                        qt_ref, kvcst_ref, kvwt_ref, selvt_ref, winvt_ref, gt_ref, *, tr):
    xn = _rms(x_ref[...], g_ref[...]).astype(BF16)
    z = jnp.dot(xn, wn_ref[...], preferred_element_type=F32)
    cmp_ref[...] = z[:, 0:512].astype(BF16)
    for p in range(2):
        selk_ref[p] = z[:, 512 + 128 * p:640 + 128 * p].astype(BF16)
        wink_ref[p] = z[:, 768 + 128 * p:896 + 128 * p].astype(BF16)
    u_ref[...] = z[:, 1024:2048]
    zt = lax.dot_general(wt_ref[...], xn, (((1,), (1,)), ((), ())), preferred_element_type=F32)
    qt_ref[...] = (zt[0:1024] * (HEAD_DIM ** -0.5 * LOG2E)).astype(BF16).reshape(N_KV, Q_PER_KV * HEAD_DIM, tr)
    kvt = zt[1024:1024 + KV_WIDTH]
    kvcst_ref[...] = kvt[0:1024]
    kvwt_ref[...] = kvt[1024:1536]
    selvt_ref[:, 0:HEAD_DIM, :] = kvt[768:1024].astype(BF16).reshape(N_KV, HEAD_DIM, tr)
    pad_row = lax.broadcasted_iota(jnp.int32, (N_KV, V_ROWS - HEAD_DIM, tr), 1)
    selvt_ref[:, HEAD_DIM:V_ROWS, :] = jnp.where(pad_row == 0, 1.0, 0.0).astype(BF16)
    winvt_ref[...] = kvt[1280:1536].astype(BF16).reshape(N_KV, HEAD_DIM, tr)
    gt_ref[...] = _sigmoid(zt[1024 + KV_WIDTH:1024 + KV_WIDTH + GATE_PAD])


def _proj_prompt(x1, g, w_nat, w_t, tr=512):
    s = x1.shape[0]
    tr = min(tr, s)
    nn, nt = w_nat.shape[1], w_t.shape[0]
    full = lambda shp: pl.BlockSpec(shp, lambda i: (0,) * len(shp))
    sds = jax.ShapeDtypeStruct
    return pl.pallas_call(
        functools.partial(_proj_prompt_kernel, tr=tr),
        grid=(s // tr,),
        in_specs=[pl.BlockSpec((tr, D_MODEL), lambda i: (i, 0)), full((1, D_MODEL)), full((D_MODEL, nn)),
                  full((nt, D_MODEL))],
        out_specs=[pl.BlockSpec((tr, 512), lambda i: (i, 0)),
                   pl.BlockSpec((2, tr, 128), lambda i: (0, i, 0)), pl.BlockSpec((2, tr, 128), lambda i: (0, i, 0)),
                   pl.BlockSpec((tr, 1024), lambda i: (i, 0)),
                   pl.BlockSpec((N_KV, 256, tr), lambda i: (0, 0, i)),
                   pl.BlockSpec((1024, tr), lambda i: (0, i)), pl.BlockSpec((512, tr), lambda i: (0, i)),
                   pl.BlockSpec((N_KV, V_ROWS, tr), lambda i: (0, 0, i)),
                   pl.BlockSpec((N_KV, HEAD_DIM, tr), lambda i: (0, 0, i)),
                   pl.BlockSpec((GATE_PAD, tr), lambda i: (0, i))],
        out_shape=[sds((s, 512), BF16), sds((2, s, 128), BF16), sds((2, s, 128), BF16),
                   sds((s, 1024), F32), sds((N_KV, 256, s), BF16), sds((1024, s), F32), sds((512, s), F32),
                   sds((N_KV, V_ROWS, s), BF16), sds((N_KV, HEAD_DIM, s), BF16), sds((GATE_PAD, s), F32)],
        compiler_params=_cparams("arbitrary"),
        name="nsa_proj_prompt",
    )(x1, g, w_nat, w_t)


def _compress_prompt_kernel(xe_ref, xo_ref, wk_ref, wvt_ref, kck_ref, kcvt_ref, acck, accv, *, tl, nh):
    step = pl.program_id(0)

    @pl.when(step == 0)
    def _():
        acck[...] = jnp.zeros_like(acck)
        accv[...] = jnp.zeros_like(accv)

    for half, x_ref in enumerate((xe_ref, xo_ref)):
        for l in range(tl):
            xk = x_ref[:, l * 512:l * 512 + 256]
            xv = x_ref[:, l * 512 + 256:l * 512 + 512]
            acck[half * nh:(half + 1) * nh, :] += jnp.dot(xk, wk_ref[l], preferred_element_type=F32)
            accv[:, half * nh:(half + 1) * nh] += lax.dot_general(
                wvt_ref[l], xv, (((1,), (1,)), ((), ())), preferred_element_type=F32)

    @pl.when(step == pl.num_programs(0) - 1)
    def _():
        for p in range(2):
            kck_ref[p] = acck[:, 128 * p:128 * (p + 1)].astype(BF16)
        kcvt_ref[...] = accv[...].astype(BF16).reshape(N_KV, HEAD_DIM, 2 * nh)


def _compress_prompt(cmp_kv, wk_bd, wvt_bd, tl=4):
    s = cmp_kv.shape[0]
    nh = s // (2 * CMP_BLOCK)
    x2 = cmp_kv.reshape(nh, 2 * CMP_BLOCK * 512)
    nsteps = CMP_BLOCK // tl
    return pl.pallas_call(
        functools.partial(_compress_prompt_kernel, tl=tl, nh=nh),
        grid=(nsteps,),
        in_specs=[pl.BlockSpec((nh, tl * 512), lambda i: (0, i)),
                  pl.BlockSpec((nh, tl * 512), lambda i: (0, nsteps + i)),
                  pl.BlockSpec((tl, 256, 256), lambda i: (i, 0, 0)),
                  pl.BlockSpec((tl, 256, 256), lambda i: (i, 0, 0))],
        out_specs=[pl.BlockSpec((2, 2 * nh, 128), lambda i: (0, 0, 0)),
                   pl.BlockSpec((N_KV, HEAD_DIM, 2 * nh), lambda i: (0, 0, 0))],
        out_shape=[jax.ShapeDtypeStruct((2, 2 * nh, 128), BF16),
                   jax.ShapeDtypeStruct((N_KV, HEAD_DIM, 2 * nh), BF16)],
        scratch_shapes=[pltpu.VMEM((2 * nh, 256), F32), pltpu.VMEM((256, 2 * nh), F32)],
        compiler_params=_cparams("arbitrary"),
        name="nsa_compress_prompt",
    )(x2, x2, wk_bd, wvt_bd)


def _softmax_rows(s):
    m = jnp.max(s, axis=0, keepdims=True)
    m = jnp.where(m == NEG_INF, 0.0, m)
    e = jnp.exp2(s - m)
    den = jnp.sum(e, axis=0, keepdims=True)
    return e / jnp.where(den > 0, den, 1.0)


def _attn_prompt_kernel(bias_ref, tblt_ref, qt_ref, kck_ref, kcvt_ref, selk_ref, selvt_ref,
                        wk0, wk1, wk2, wk3, wk4, wv0, wv1, wv2, wv3, wv4,
                        gt_ref, u_ref, x1_ref, wout_ref, gfin_ref, y_ref,
                        nb_s, wpen_s, tbl_s, wq_s, pen_s, sc_s, s0_s, obr_s, opt_s, *, s_len):
    i = pl.program_id(0)
    q0 = i * Q_BLOCK
    nc = s_len // CMP_BLOCK
    nch = nc // 2
    nblk = s_len // SEL_BLOCK
    k_top = min(TOP_N, nblk)
    wrows = min(32, nch)
    wk_refs = (wk0, wk1, wk2, wk3, wk4)
    wv_refs = (wv0, wv1, wv2, wv3, wv4)
    n_win = len(wk_refs)

    def table(h):
        return lambda b: (bias_ref[b, h] - bias_ref[N_BUCKETS - 1, h]) * LOG2E

    @pl.when(i == 0)
    def _():
        wq_s[...] = jnp.zeros_like(wq_s)
        rho = lax.broadcasted_iota(jnp.int32, (NEAR_ROWS, Q_BLOCK), 0)
        a = lax.broadcasted_iota(jnp.int32, (NEAR_ROWS, Q_BLOCK), 1)
        dist = a + (MAX_DISTANCE + NEAR_PAD) - rho
        for h in range(N_HEADS):
            nb_s[h] = _bias_from_dist(dist, table(h))
        loc_w = lax.broadcasted_iota(jnp.int32, (n_win * Q_BLOCK, Q_BLOCK), 0)
        d_w = lax.broadcasted_iota(jnp.int32, (n_win * Q_BLOCK, Q_BLOCK), 1) + WINDOW - loc_w
        wpen_s[...] = jnp.where((d_w >= 0) & (d_w < WINDOW), 0.0, NEG_INF)
        t = tblt_ref[...]
        ts = (t - t[:, N_BUCKETS - 1:N_BUCKETS]) * LOG2E
        for h in range(N_HEADS):
            tbl_s[h] = jnp.broadcast_to(ts[h:h + 1, :], (8, 128))

    for g in range(N_KV):
        off = (g % 2) * HEAD_DIM
        for r in range(Q_PER_KV):
            wq_s[g, off:off + HEAD_DIM, r * Q_BLOCK:(r + 1) * Q_BLOCK] = qt_ref[g, r * HEAD_DIM:(r + 1) * HEAD_DIM, :]

    lane_q = lax.broadcasted_iota(jnp.int32, (1, LANES_G), 1) % Q_BLOCK
    t_lane = q0 + lane_q
    blk = lax.broadcasted_iota(jnp.int32, (nblk, Q_BLOCK), 0)
    tq = lax.broadcasted_iota(jnp.int32, (nblk, Q_BLOCK), 1) + q0
    forced = (blk == 0) | (blk == tq // SEL_BLOCK)
    valid = blk * SEL_BLOCK <= tq
    blk_f = blk.astype(F32)

    w0 = jnp.clip(((2 * i - 16) // 8) * 8, 0, nch - wrows)
    w0 = pl.multiple_of(w0, 8)
    jj = lax.broadcasted_iota(jnp.int32, (wrows, Q_BLOCK), 0) + w0
    aq = lax.broadcasted_iota(jnp.int32, (wrows, Q_BLOCK), 1) + q0
    buckets = []
    for half in range(2):
        dist = aq - SEL_BLOCK * jj - (CMP_BLOCK * half + CMP_BLOCK - 1)
        bucket = jnp.zeros((wrows, Q_BLOCK), jnp.int32)
        for lo in _BUCKET_LOWS:
            bucket = bucket + (dist >= lo).astype(jnp.int32)
        buckets.append(bucket)

    def group_body(g, carry):
        wq = wq_s[g]
        pair = g // 2

        sc_s[...] = jnp.dot(kck_ref[pair], wq, preferred_element_type=F32)
        for r in range(Q_PER_KV):
            trow = jnp.concatenate([tbl_s[g * Q_PER_KV + r]] * (wrows // 8), axis=0)
            for half in range(2):
                rows = pl.ds(pl.multiple_of(half * nch + w0, 8), wrows)
                sc_s[rows, r * Q_BLOCK:(r + 1) * Q_BLOCK] += jnp.take_along_axis(trow, buckets[half], axis=1)
        rho = lax.broadcasted_iota(jnp.int32, (nc, 1), 0)
        ends = jnp.where(rho >= nch, SEL_BLOCK * (rho - nch) + 2 * CMP_BLOCK - 1, SEL_BLOCK * rho + CMP_BLOCK - 1)
        p_c = _softmax_rows(jnp.where(ends <= t_lane, sc_s[...], NEG_INF))
        obr_s[0, g] = jnp.dot(kcvt_ref[g], p_c.astype(BF16), preferred_element_type=F32)

        psum = p_c[:, 0:Q_BLOCK]
        for r in range(1, Q_PER_KV):
            psum = psum + p_c[:, r * Q_BLOCK:(r + 1) * Q_BLOCK]
        imp = psum[0:nch] + psum[nch:nc]
        pen_s[g, :, 0:Q_BLOCK] = jnp.where(valid & ~forced, imp, NEG_INF)

        kw = jnp.concatenate([wk_refs[k][pair] for k in range(n_win)], axis=0)
        s = jnp.dot(kw, wq, preferred_element_type=F32)
        nw = n_win * Q_BLOCK
        r_w = NEAR_PAD + MAX_DISTANCE - WINDOW
        band = jnp.concatenate(
            [wpen_s[k * Q_BLOCK:(k + 1) * Q_BLOCK, :] + jnp.where(i - (n_win - 1) + k >= 0, 0.0, NEG_INF)
             for k in range(n_win)], axis=0)
        bias = jnp.concatenate([nb_s[g * Q_PER_KV + r, r_w:r_w + nw, :] + band for r in range(Q_PER_KV)], axis=1)
        p_w = _softmax_rows(s + bias)
        vw = jnp.concatenate([wv_refs[k][g] for k in range(n_win)], axis=1)
        obr_s[2, g] = jnp.dot(vw, p_w.astype(BF16), preferred_element_type=F32)
        return carry

    lax.fori_loop(0, N_KV, group_body, 0)

    def pick(_, c):
        for g in range(N_KV):
            sc = pen_s[g, :, 0:Q_BLOCK]
            m = jnp.max(sc, axis=0, keepdims=True)
            idx = jnp.min(jnp.where(sc == m, blk_f, float(nblk)), axis=0, keepdims=True)
            pen_s[g, :, 0:Q_BLOCK] = jnp.where(blk_f == idx, NEG_INF, sc)
        return c

    lax.fori_loop(0, k_top - 2, pick, 0)
    for g in range(N_KV):
        picked = (pen_s[g, :, 0:Q_BLOCK] == NEG_INF) & (forced | valid)
        pen = jnp.where(picked, 0.0, NEG_INF)
        pen_s[g] = jnp.concatenate([pen] * Q_PER_KV, axis=1)

    n_tiles = (q0 + Q_BLOCK - 1) // KEY_TILE + 1
    n_far = jnp.maximum(q0 - MAX_DISTANCE, 0) // KEY_TILE
    nbt = KEY_TILE // SEL_BLOCK

    def qk(g, tt):
        k0 = pl.multiple_of(tt * KEY_TILE, KEY_TILE)
        return jnp.dot(selk_ref[g // 2, pl.ds(k0, KEY_TILE), :], wq_s[g], preferred_element_type=F32)

    def tile(tt, state, kind):
        k0 = pl.multiple_of(tt * KEY_TILE, KEY_TILE)
        if kind != "far":
            r0 = pl.multiple_of(jnp.maximum(k0 - q0 + MAX_DISTANCE + NEAR_PAD, 0), Q_BLOCK)
        if kind == "diag":
            causal = k0 + lax.broadcasted_iota(jnp.int32, (KEY_TILE, 1), 0) <= t_lane
        out = []
        s_next = s0_s[...]
        for g in range(N_KV):
            s = s_next
            if g + 1 < N_KV:
                s_next = qk(g + 1, tt)
            elif kind != "diag":
                s_next = qk(0, tt + 1)
            m_old, acc = state[g]
            pen_t = pen_s[g, pl.ds(pl.multiple_of(tt * nbt, nbt), nbt), :]
            if kind != "far":
                s = s + jnp.concatenate(
                    [nb_s[g * Q_PER_KV + r, pl.ds(r0, KEY_TILE), :] for r in range(Q_PER_KV)], axis=1)
            if kind == "diag":
                s = jnp.where(causal, s, NEG_INF)
            else:
                s = s.astype(BF16)
                pen_t = pen_t.astype(BF16)
            s = (s.reshape(nbt, SEL_BLOCK, LANES_G) + pen_t[:, None, :]).reshape(KEY_TILE, LANES_G)
            m_new = jnp.maximum(m_old, jnp.max(s, axis=0, keepdims=True).astype(F32))
            alpha = jnp.exp2(m_old - m_new)
            p = jnp.exp2(s - m_new.astype(s.dtype)).astype(BF16)
            pv = jnp.dot(selvt_ref[g, :, pl.ds(k0, KEY_TILE)], p, preferred_element_type=F32)
            out.append((m_new, alpha * acc + pv))
        if kind != "diag":
            s0_s[...] = s_next
        return tuple(out)

    s0_s[...] = qk(0, 0)
    init = tuple((jnp.full((1, LANES_G), M_FLOOR, F32), jnp.zeros((V_ROWS, LANES_G), F32)) for _ in range(N_KV))
    far_done = lax.fori_loop(0, n_far, lambda tt, c: tile(tt, c, "far"), init)
    mid_done = lax.fori_loop(n_far, n_tiles - 1, lambda tt, c: tile(tt, c, "mid"), far_done)
    sel_done = tile(n_tiles - 1, mid_done, "diag")
    for g in range(N_KV):
        acc = sel_done[g][1]
        obr_s[1, g] = acc[0:HEAD_DIM] / acc[HEAD_DIM:HEAD_DIM + 1]

    for g in range(N_KV):
        for r in range(Q_PER_KV):
            h = g * Q_PER_KV + r
            sl = slice(r * Q_BLOCK, (r + 1) * Q_BLOCK)
            merged = (gt_ref[h:h + 1, :] * obr_s[0, g, :, sl] + gt_ref[N_HEADS + h:N_HEADS + h + 1, :] * obr_s[1, g, :, sl]
                      + gt_ref[2 * N_HEADS + h:2 * N_HEADS + h + 1, :] * obr_s[2, g, :, sl])
            opt_s[h * HEAD_DIM:(h + 1) * HEAD_DIM, :] = merged
    op = opt_s[...].T
    y = jnp.dot((op * _silu(u_ref[...])).astype(BF16), wout_ref[...], preferred_element_type=F32)
    y_ref[...] = _rms(x1_ref[...] + y, gfin_ref[...])


def _attn_prompt(rel_bias, qt, kck, kcvt, selk, selvt, wink, winvt, gt, u, x1, w_out, gfin):
    s = x1.shape[0]
    nq = s // Q_BLOCK
    nc = s // CMP_BLOCK
    nblk = s // SEL_BLOCK
    assert s % (2 * CMP_BLOCK * 128) == 0 and nblk >= TOP_N
    n_win = WINDOW // Q_BLOCK + 1
    full = lambda shp, **kw: pl.BlockSpec(shp, lambda i: (0,) * len(shp), **kw)
    once = dict(pipeline_mode=pl.Buffered(1))
    win_k_specs = [pl.BlockSpec((2, Q_BLOCK, 128), functools.partial(lambda i, k: (0, jnp.maximum(i - (n_win - 1) + k, 0), 0), k=k))
                   for k in range(n_win)]
    win_v_specs = [pl.BlockSpec((N_KV, HEAD_DIM, Q_BLOCK), functools.partial(lambda i, k: (0, 0, jnp.maximum(i - (n_win - 1) + k, 0)), k=k))
                   for k in range(n_win)]
    return pl.pallas_call(
        functools.partial(_attn_prompt_kernel, s_len=s),
        grid=(nq,),
        in_specs=[pl.BlockSpec(memory_space=pltpu.SMEM), full((N_HEADS, 128)),
                  pl.BlockSpec((N_KV, 256, Q_BLOCK), lambda i: (0, 0, i)),
                  full((2, nc, 128), **once), full((N_KV, HEAD_DIM, nc), **once),
                  full((2, s, 128), **once), full((N_KV, V_ROWS, s), **once),
                  *win_k_specs, *win_v_specs,
                  pl.BlockSpec((GATE_PAD, Q_BLOCK), lambda i: (0, i)),
                  pl.BlockSpec((Q_BLOCK, 1024), lambda i: (i, 0)),
                  pl.BlockSpec((Q_BLOCK, D_MODEL), lambda i: (i, 0)),
                  full((1024, D_MODEL), **once), full((1, D_MODEL))],
        out_specs=pl.BlockSpec((Q_BLOCK, D_MODEL), lambda i: (i, 0)),
        out_shape=jax.ShapeDtypeStruct((s, D_MODEL), F32),
        scratch_shapes=[pltpu.VMEM((N_HEADS, NEAR_ROWS, Q_BLOCK), F32),
                        pltpu.VMEM((n_win * Q_BLOCK, Q_BLOCK), F32),
                        pltpu.VMEM((N_HEADS, 8, 128), F32),
                        pltpu.VMEM((N_KV, 128, LANES_G), BF16),
                        pltpu.VMEM((N_KV, nblk, LANES_G), F32),
                        pltpu.VMEM((nc, LANES_G), F32),
                        pltpu.VMEM((KEY_TILE, LANES_G), F32),
                        pltpu.VMEM((3, N_KV, HEAD_DIM, LANES_G), F32),
                        pltpu.VMEM((N_HEADS * HEAD_DIM, Q_BLOCK), F32)],
        compiler_params=_cparams("arbitrary"),
        name="nsa_attn_prompt",
    )(rel_bias, jnp.pad(rel_bias.T, ((0, 0), (0, 128 - N_BUCKETS))), qt, kck, kcvt, selk, selvt,
      *([wink] * n_win), *([winvt] * n_win), gt, u, x1, w_out, gfin)


PAGES_PER_STEP = 16
CHUNK_PAGES = 32
STEPS_PER_CHUNK = CHUNK_PAGES // PAGES_PER_STEP
NEAR_PAGES = MAX_DISTANCE // PAGE_SIZE
NEW_PAD = 128


def _proj_sample_kernel(x_ref, g_ref, wn_ref, kvcs_ref, kvw_ref, q_ref, u_ref, gate_ref):
    xn = _rms(x_ref[...], g_ref[...]).astype(BF16)
    z = jnp.dot(xn, wn_ref[...], preferred_element_type=F32)
    kvcs_ref[...] = z[:, 0:1024]
    kvw_ref[...] = z[:, 1024:1536]
    q_ref[...] = (z[:, 1536:2560] * (HEAD_DIM ** -0.5)).astype(BF16)
    u_ref[...] = z[:, 2560:3584]
    gate_ref[...] = _sigmoid(z[:, 3584:3584 + GATE_PAD])


def _proj_sample(x1, g, w_nat):
    rows = x1.shape[0]
    sds = jax.ShapeDtypeStruct
    return pl.pallas_call(
        _proj_sample_kernel,
        out_shape=[sds((rows, 1024), F32), sds((rows, 512), F32), sds((rows, 1024), BF16),
                   sds((rows, 1024), F32), sds((rows, GATE_PAD), F32)],
        compiler_params=pltpu.CompilerParams(vmem_limit_bytes=VMEM_LIMIT),
        name="nsa_proj_sample",
    )(x1, g, w_nat)


def _softmax_lanes_online(s, m_old, l_old):
    m_new = jnp.maximum(m_old, jnp.max(s, axis=1, keepdims=True))
    alpha = jnp.exp(m_old - m_new)
    p = jnp.exp(s - m_new)
    return m_new, alpha, alpha * l_old + jnp.sum(p, axis=1, keepdims=True), p


def _attn_sample_kernel(pt_ref, *refs, past, ds, wb):
    pages = refs[:PAGES_PER_STEP]
    (q_ref, new_ref, wint_ref, gate_ref, u_ref, x1_ref, perm_ref, wk_ref, wv_ref, tb_ref, wout_ref, gfin_ref, y_ref,
     xl_s, kckt_s, kcv_s, skt_s, svt_s, bc_s, bs_s, bnew_s, bw_s, obr_s) = refs[PAGES_PER_STEP:]
    b = pl.program_id(0)
    pg = pl.program_id(1)
    npg = pl.num_programs(1)
    rows = Q_PER_KV * N_KV * ds
    nchunk = past // (CHUNK_PAGES * PAGE_SIZE)
    npages = past // PAGE_SIZE
    ncl = past // CMP_BLOCK
    nblk = past // SEL_BLOCK
    k_top = min(TOP_N - 1, nblk)
    nt = (((1,), (1,)), ((), ()))

    row_i = lax.broadcasted_iota(jnp.int32, (rows, 1), 0)
    tt = row_i % ds

    @pl.when((b == 0) & (pg == 0))
    def _():
        tbs = tb_ref[...] - tb_ref[:, N_BUCKETS - 1:N_BUCKETS]
        col = lambda k: tbs[:, k:k + 1]
        lane = lax.broadcasted_iota(jnp.int32, (1, ncl), 1)
        slot = lane % 8
        cblk = 8 * (lane // 8) + jnp.where(slot < 4, 2 * slot, 2 * (slot - 4) + 1)
        bc_s[...] = _bias_from_dist(past + tt - (CMP_BLOCK * cblk + CMP_BLOCK - 1), col)
        lane_p = lax.broadcasted_iota(jnp.int32, (1, PAGE_SIZE), 1)
        for k in range(NEAR_PAGES):
            bs_s[k] = _bias_from_dist(tt + MAX_DISTANCE - PAGE_SIZE * k - lane_p, col)
        bnew_s[...] = _bias_from_dist(tt - lax.broadcasted_iota(jnp.int32, (1, NEW_PAD), 1), col)
        bw_s[...] = _bias_from_dist(tt + wb - lax.broadcasted_iota(jnp.int32, (1, wb), 1), col)

    for pair in range(PAGES_PER_STEP // 2):
        pa, pb = pages[2 * pair], pages[2 * pair + 1]
        slab = pl.ds(pl.multiple_of(((pg % STEPS_PER_CHUNK) * (PAGES_PER_STEP // 2) + pair) * 8, 8), 8)
        for kind in range(2):
            xt2 = jnp.concatenate([pa[0, kind * 256:(kind + 1) * 256, :], pb[0, kind * 256:(kind + 1) * 256, :]],
                                  axis=1).astype(BF16)
            x_perm = lax.dot_general(perm_ref[...], xt2, nt, preferred_element_type=F32)
            xl_s[:, slab, kind * 256:(kind + 1) * 256] = x_perm.reshape(CMP_BLOCK, 8, 256)
    for k in range(PAGES_PER_STEP):
        skt_s[pg * PAGES_PER_STEP + k] = pages[k][0, 512:768, :].astype(BF16)
        svt_s[pg * PAGES_PER_STEP + k] = pages[k][0, 768:1024, :].astype(BF16)

    @pl.when(pg % STEPS_PER_CHUNK == STEPS_PER_CHUNK - 1)
    def _():
        nbc = CHUNK_PAGES * PAGE_SIZE // CMP_BLOCK

        def body(l, acc):
            ak, av = acc
            xk = xl_s[l, :, 0:256].astype(BF16)
            xv = xl_s[l, :, 256:512].astype(BF16)
            ak = ak + jnp.dot(xk, wk_ref[l], preferred_element_type=F32)
            av = av + jnp.dot(xv, wv_ref[l], preferred_element_type=F32)
            return ak, av

        ak, av = lax.fori_loop(0, CMP_BLOCK, body, (jnp.zeros((nbc, 256), F32), jnp.zeros((nbc, 256), F32)),
                               unroll=4)
        ch = pg // STEPS_PER_CHUNK
        kckt_s[ch] = ak.T.astype(BF16)
        kcv_s[ch] = av.astype(BF16)

    @pl.when(pg == npg - 1)
    def _():
        lane_g = lax.broadcasted_iota(jnp.int32, (1, 256), 1) // HEAD_DIM
        row_g = (row_i // ds) % N_KV
        wq = jnp.where(row_g == lane_g, q_ref[0], jnp.zeros((), BF16))
        t_row = past + tt

        s_c = jnp.concatenate([jnp.dot(wq, kckt_s[ch], preferred_element_type=F32) for ch in range(nchunk)],
                              axis=1) + bc_s[...]
        m = jnp.max(s_c, axis=1, keepdims=True)
        e = jnp.exp(s_c - m)
        p_c = e / jnp.sum(e, axis=1, keepdims=True)
        o_c = jnp.zeros((rows, 256), F32)
        for ch in range(nchunk):
            o_c = o_c + jnp.dot(p_c[:, ch * 128:(ch + 1) * 128].astype(BF16), kcv_s[ch], preferred_element_type=F32)
        obr_s[0] = o_c

        gt_rows = N_KV * ds
        ps = p_c[0:gt_rows]
        for r in range(1, Q_PER_KV):
            ps = ps + p_c[r * gt_rows:(r + 1) * gt_rows]
        imp = ps + pltpu.roll(ps, ncl - 4, axis=1)
        lane = lax.broadcasted_iota(jnp.int32, (gt_rows, ncl), 1)
        slot = lane % 8
        jblk = 4 * (lane // 8) + slot
        t_gt = past + lax.broadcasted_iota(jnp.int32, (gt_rows, ncl), 0) % ds
        forced = (jblk == 0) | (jblk == t_gt // SEL_BLOCK)
        valid = jblk * SEL_BLOCK <= t_gt
        scores = jnp.where(slot < 4, jnp.where(forced, jnp.inf, jnp.where(valid, imp, NEG_INF)), NEG_INF)
        jf = jnp.where(slot < 4, jblk, 2 * nblk).astype(F32)
        picks = []
        for _ in range(k_top):
            mx = jnp.max(scores, axis=1, keepdims=True)
            idx = jnp.min(jnp.where(scores == mx, jf, float(2 * nblk)), axis=1, keepdims=True)
            scores = jnp.where(jf == idx, NEG_INF, scores)
            picks.append(idx)

        gk = NEAR_PAGES
        blk_lane = (lax.broadcasted_iota(jnp.int32, (1, gk * PAGE_SIZE), 1) // SEL_BLOCK).astype(F32)

        def group_logits(pgi):
            return jnp.concatenate([jnp.dot(wq, skt_s[pgi * gk + k], preferred_element_type=F32) for k in range(gk)],
                                   axis=1)

        def pages_step(pgi, carry, bias, s=None):
            m_old, l_old, acc = carry
            if s is None:
                s = group_logits(pgi)
            if bias is not None:
                s = s + bias
            jk = blk_lane + jnp.asarray(pgi * (gk * PAGE_SIZE // SEL_BLOCK), F32)
            hit = jk == picks[0]
            for pk in picks[1:]:
                hit = hit | (jk == pk)
            pen = jnp.where(hit, 0.0, NEG_INF)
            s = (s.reshape(Q_PER_KV, gt_rows, gk * PAGE_SIZE) + pen[None]).reshape(rows, gk * PAGE_SIZE)
            m_new, alpha, l_new, pr = _softmax_lanes_online(s, m_old, l_old)
            prb = pr.astype(BF16)
            pv = alpha * acc
            for k in range(gk):
                pv = pv + lax.dot_general(prb[:, k * PAGE_SIZE:(k + 1) * PAGE_SIZE], svt_s[pgi * gk + k], nt,
                                          preferred_element_type=F32)
            return m_new, l_new, pv

        empty = (jnp.full((rows, 1), M_FLOOR, F32), jnp.zeros((rows, 1), F32), jnp.zeros((rows, 256), F32))
        n_far_groups = npages // gk - 1
        n_pairs = n_far_groups // 2

        def pair_step(j, chains):
            ca, cb = chains
            sa, sb = group_logits(j), group_logits(j + n_pairs)
            return pages_step(j, ca, None, sa), pages_step(j + n_pairs, cb, None, sb)

        ca, cb = lax.fori_loop(0, n_pairs, pair_step, (empty, empty))
        for j in range(2 * n_pairs, n_far_groups):
            ca = pages_step(j, ca, None)
        m_far = jnp.maximum(ca[0], cb[0])
        wa, wb_ = jnp.exp(ca[0] - m_far), jnp.exp(cb[0] - m_far)
        carry = (m_far, wa * ca[1] + wb_ * cb[1], wa * ca[2] + wb_ * cb[2])
        bias_near = jnp.concatenate([bs_s[k] for k in range(NEAR_PAGES)], axis=1)
        m_old, l_old, acc = pages_step(n_far_groups, carry, bias_near)
        lane_n = lax.broadcasted_iota(jnp.int32, (1, NEW_PAD), 1)
        ok_new = (lane_n < ds) & (lane_n <= tt)
        s = lax.dot_general(wq, new_ref[0, 0], nt, preferred_element_type=F32) + bnew_s[...]
        m_new, alpha, l_new, pr = _softmax_lanes_online(jnp.where(ok_new, s, NEG_INF), m_old, l_old)
        acc = alpha * acc + jnp.dot(pr.astype(BF16), new_ref[0, 1], preferred_element_type=F32)
        obr_s[1] = acc / l_new

        kwt = wint_ref[0, 0:256, :].astype(BF16)
        vwt = wint_ref[0, 256:512, :].astype(BF16)
        dist = tt + wb - lax.broadcasted_iota(jnp.int32, (1, wb), 1)
        s_w = jnp.where((dist >= 0) & (dist < WINDOW), jnp.dot(wq, kwt, preferred_element_type=F32) + bw_s[...], NEG_INF)
        s_n = jnp.where(ok_new, lax.dot_general(wq, new_ref[0, 2], nt, preferred_element_type=F32) + bnew_s[...], NEG_INF)
        m = jnp.maximum(jnp.max(s_w, axis=1, keepdims=True), jnp.max(s_n, axis=1, keepdims=True))
        e_w = jnp.exp(s_w - m)
        e_n = jnp.exp(s_n - m)
        den = jnp.sum(e_w, axis=1, keepdims=True) + jnp.sum(e_n, axis=1, keepdims=True)
        o_w = (lax.dot_general(e_w.astype(BF16), vwt, nt, preferred_element_type=F32)
               + jnp.dot(e_n.astype(BF16), new_ref[0, 3], preferred_element_type=F32))
        obr_s[2] = o_w / den

        merged = []
        for r in range(Q_PER_KV):
            tot = None
            for br in range(3):
                o_r = jnp.zeros((ds, 256), F32)
                g_r = jnp.zeros((ds, 256), F32)
                for g in range(N_KV):
                    base = r * gt_rows + g * ds
                    o_r = jnp.where(lane_g == g, obr_s[br, base:base + ds, :], o_r)
                    c = br * N_HEADS + g * Q_PER_KV + r
                    g_r = jnp.where(lane_g == g, gate_ref[0, :, c:c + 1], g_r)
                tot = g_r * o_r if tot is None else tot + g_r * o_r
            merged.append(tot)
        op = jnp.concatenate(merged, axis=1)
        y = jnp.dot((op * _silu(u_ref[0])).astype(BF16), wout_ref[...], preferred_element_type=F32)
        y_ref[0] = _rms(x1_ref[0] + y, gfin_ref[...])


def _attn_sample(page_table, pool_t, q64, newkv, win_t, gates, u_r, x1, perm, wkt_bd, wv_bd, tb_rows, w_out_r, gfin,
                 *, past, ds, wb):
    nb = page_table.shape[0]
    npages = past // PAGE_SIZE
    assert past % (CHUNK_PAGES * PAGE_SIZE) == 0 and npages >= NEAR_PAGES and past // SEL_BLOCK >= TOP_N
    npg = npages // PAGES_PER_STEP
    nchunk = npages // CHUNK_PAGES
    rows = Q_PER_KV * N_KV * ds
    ncl = past // CMP_BLOCK
    once = dict(pipeline_mode=pl.Buffered(1))
    full = lambda shp, **kw: pl.BlockSpec(shp, lambda b, p, pt: (0,) * len(shp), **kw)
    per_b = lambda shp: pl.BlockSpec((1,) + shp, lambda b, p, pt: (b,) + (0,) * len(shp))
    page_specs = [pl.BlockSpec((1, 1024, PAGE_SIZE),
                               functools.partial(lambda b, p, pt, k: (pt[b, p * PAGES_PER_STEP + k], 0, 0), k=k))
                  for k in range(PAGES_PER_STEP)]
    grid_spec = pltpu.PrefetchScalarGridSpec(
        num_scalar_prefetch=1, grid=(nb, npg),
        in_specs=[*page_specs, per_b((rows, 256)), per_b((4, NEW_PAD, 256)), per_b((512, wb)),
                  per_b((ds, GATE_PAD)), per_b((ds, 1024)), per_b((ds, D_MODEL)),
                  full((256, 256), **once), full((CMP_BLOCK, 256, 256), **once), full((CMP_BLOCK, 256, 256), **once),
                  full((rows, N_BUCKETS)), full((1024, D_MODEL), **once), full((1, D_MODEL))],
        out_specs=pl.BlockSpec((1, ds, D_MODEL), lambda b, p, pt: (b, 0, 0)),
        scratch_shapes=[pltpu.VMEM((CMP_BLOCK, CHUNK_PAGES * PAGE_SIZE // CMP_BLOCK, 512), F32),
                        pltpu.VMEM((nchunk, 256, 128), BF16), pltpu.VMEM((nchunk, 128, 256), BF16),
                        pltpu.VMEM((npages, 256, PAGE_SIZE), BF16), pltpu.VMEM((npages, 256, PAGE_SIZE), BF16),
                        pltpu.VMEM((rows, ncl), F32), pltpu.VMEM((NEAR_PAGES, rows, PAGE_SIZE), F32),
                        pltpu.VMEM((rows, NEW_PAD), F32), pltpu.VMEM((rows, wb), F32),
                        pltpu.VMEM((3, rows, 256), F32)])
    return pl.pallas_call(
        functools.partial(_attn_sample_kernel, past=past, ds=ds, wb=wb),
        grid_spec=grid_spec,
        out_shape=jax.ShapeDtypeStruct((nb, ds, D_MODEL), F32),
        compiler_params=_cparams("arbitrary", "arbitrary"),
        name="nsa_attn_sample",
    )(page_table, *([pool_t] * PAGES_PER_STEP), q64, newkv, win_t, gates, u_r, x1, perm, wkt_bd, wv_bd, tb_rows,
      w_out_r, gfin)


def _pair_permutation():
    m = jnp.arange(256)
    l, slot = m // 8, m % 8
    cb = jnp.where(slot < 4, 2 * slot, 2 * (slot - 4) + 1)
    tok = CMP_BLOCK * cb + l
    return (tok[:, None] == jnp.arange(256)[None, :]).astype(BF16)


def _block_diag4(w):
    eye = jnp.eye(N_KV, dtype=w.dtype)
    out = eye[:, None, :, None] * w[..., None, :, None, :]
    return out.reshape(*w.shape[:-2], N_KV * HEAD_DIM, N_KV * HEAD_DIM)


def kernel(x_prompt, x_sample, state_lru_h, state_conv, cache_nsa_kv, cache_win_kv, page_table, norm_g,
           final_norm_g, w_in_lru, conv_w, conv_b, w_gate_a, b_gate_a, w_gate_x, b_gate_x, lru_lambda,
           w_out_lru, w_in_nsa, w_cmp, w_out_nsa, rel_bias):
    bp, s_len, _ = x_prompt.shape
    nb, ds, _ = x_sample.shape
    assert bp == 1
    row = lambda v: v.reshape(1, -1)

    lru_w = (row(norm_g[0]), w_in_lru[0].astype(BF16), conv_w[0], row(conv_b[0]), w_gate_a[0].astype(BF16),
             row(b_gate_a[0]), w_gate_x[0].astype(BF16), row(b_gate_x[0]), row(lru_lambda[0]),
             w_out_lru[0].astype(BF16))
    x1p, hp, tailp = _lru_prompt(x_prompt[0], *lru_w)
    xs_tm = x_sample.transpose(1, 0, 2).reshape(ds * nb, D_MODEL)
    buf_tm = state_conv[0].transpose(1, 0, 2).reshape((CONV_W - 1) * nb, D_MODEL)
    x1s_tm, hs_last, tails = _lru_sample(xs_tm, buf_tm, state_lru_h[0], *lru_w, nb=nb, ds=ds)

    prompt_lru_h = hp.reshape(1, 1, D_MODEL)
    prompt_conv = tailp[8 - (CONV_W - 1):].reshape(1, 1, CONV_W - 1, D_MODEL)
    sample_lru_h = hs_last.reshape(1, nb, D_MODEL)
    sample_conv = tails.reshape(CONV_W - 1, nb, D_MODEL).transpose(1, 0, 2)[None]

    w = w_in_nsa[0]
    wq, wkv = w[:, :1024], w[:, 1024:1024 + KV_WIDTH]
    wg, wu = w[:, 1024 + KV_WIDTH:1024 + KV_WIDTH + 3 * N_HEADS], w[:, 1024 + KV_WIDTH + 3 * N_HEADS:]
    wg_pad = jnp.pad(wg, ((0, 0), (0, GATE_PAD - 3 * N_HEADS)))
    w_nat_p = jnp.concatenate([wkv[:, 0:768], wkv[:, 1024:1280], wu], axis=1).astype(BF16)
    w_t_p = jnp.concatenate([wq.T, wkv.T, wg_pad.T], axis=0).astype(BF16)
    wk_bd = _block_diag4(w_cmp[0, 0]).astype(BF16)
    wvt_bd = jnp.swapaxes(_block_diag4(w_cmp[0, 1]), -1, -2).astype(BF16)
    w_out = w_out_nsa[0].astype(BF16)
    g1 = row(norm_g[1])
    gfin = row(final_norm_g)

    cmp_kv, selk, wink, u_p, qt, kvcst, kvwt, selvt, winvt, gt = _proj_prompt(x1p, g1, w_nat_p, w_t_p)
    kck, kcvt = _compress_prompt(cmp_kv, wk_bd, wvt_bd)
    y_prompt = _attn_prompt(rel_bias, qt, kck, kcvt, selk, selvt, wink, winvt, gt, u_p, x1p, w_out, gfin)
    y_prompt = y_prompt[None]
    prompt_kv_rows = kvcst.reshape(4, N_KV, HEAD_DIM, s_len).transpose(3, 0, 1, 2)[None, None]
    wbp = min(WINDOW, s_len)
    prompt_win_kv = kvwt[:, s_len - wbp:].reshape(2, N_KV, HEAD_DIM, wbp).transpose(3, 0, 1, 2)[None, None]

    n_phys, n_layers = cache_nsa_kv.shape[:2]
    past = page_table.shape[1] * PAGE_SIZE
    wb = cache_win_kv.shape[2]
    to_rgd = lambda m: m.reshape(m.shape[0], N_KV, Q_PER_KV, HEAD_DIM).transpose(0, 2, 1, 3).reshape(m.shape[0], 1024)
    w_nat_s = jnp.concatenate([wkv, to_rgd(wq), to_rgd(wu), wg_pad], axis=1).astype(BF16)
    w_out_r = to_rgd(w_out_nsa[0].T).T.astype(BF16)
    x1s = x1s_tm.reshape(ds, nb, D_MODEL).transpose(1, 0, 2).reshape(nb * ds, D_MODEL)
    kvcs_s, kvw_s, q_s, u_s, gate_s = _proj_sample(x1s, g1, w_nat_s)
    q64 = q_s.reshape(nb, ds, Q_PER_KV, 1, 256).transpose(0, 2, 3, 1, 4)
    q64 = jnp.broadcast_to(q64, (nb, Q_PER_KV, N_KV, ds, 256)).reshape(nb, Q_PER_KV * N_KV * ds, 256)
    new4 = jnp.stack([kvcs_s[:, 512:768], kvcs_s[:, 768:1024], kvw_s[:, 0:256], kvw_s[:, 256:512]], axis=0)
    new4 = new4.reshape(4, nb, ds, 256).transpose(1, 0, 2, 3).astype(BF16)
    newkv = jnp.pad(new4, ((0, 0), (0, 0), (0, NEW_PAD - ds), (0, 0)))
    pool_t = cache_nsa_kv.transpose(0, 1, 3, 4, 5, 2).reshape(n_phys * n_layers, 1024, PAGE_SIZE)
    win_t = cache_win_kv[0].transpose(0, 2, 3, 4, 1).reshape(nb, 512, wb)
    head_of_row = (jnp.arange(Q_PER_KV * N_KV * ds) // ds % N_KV) * Q_PER_KV + jnp.arange(Q_PER_KV * N_KV * ds) // (N_KV * ds)
    tb_rows = rel_bias.T[head_of_row]
    y_sample = _attn_sample(page_table * n_layers, pool_t, q64, newkv, win_t, gate_s.reshape(nb, ds, GATE_PAD),
                            u_s.reshape(nb, ds, 1024), x1s.reshape(nb, ds, D_MODEL), _pair_permutation(),
                            wk_bd, jnp.swapaxes(wvt_bd, -1, -2), tb_rows, w_out_r, gfin,
                            past=past, ds=ds, wb=wb)
    sample_kv_rows = kvcs_s.reshape(nb, 1, ds, 4, N_KV, HEAD_DIM)
    new_t = kvw_s.reshape(nb, ds, 512).transpose(0, 2, 1)
    win_all = jnp.concatenate([win_t, new_t], axis=2)[:, :, ds:]
    sample_win_kv = win_all.reshape(nb, 2, N_KV, HEAD_DIM, wb).transpose(0, 4, 1, 2, 3)[None]

    return (y_prompt, y_sample, prompt_lru_h, prompt_conv, prompt_kv_rows, prompt_win_kv,
            sample_lru_h, sample_conv, sample_kv_rows, sample_win_kv)
```
